```python
import jax, jax.numpy as jnp
from jax import lax
import numpy as np

D_MODEL = 1024
BATCH = 4
SEQ = 4096
DEPTH = 2

GRID_W = 64
CTX_LEN = 256
MIX_WIDTH = D_MODEL // 2
HEAD_DIM = 64
GQA_HEADS = MIX_WIDTH // HEAD_DIM
GQA_KV_HEADS = GQA_HEADS // 4
NA_HEADS = MIX_WIDTH // HEAD_DIM
NA_WIN_ROWS = 8
NA_WIN_COLS = 16
FOURIER_WIDTH = MIX_WIDTH
FOURIER_GROUPS = 4
FOURIER_GROUP_DIM = FOURIER_WIDTH // FOURIER_GROUPS
CONV_WIDTH = MIX_WIDTH
CONV_KERNEL = 31
N_BRANCHES = 4
N_EXPERTS = 16
N_EXPERT_GROUPS = 4
EXPERTS_PER_GROUP = N_EXPERTS // N_EXPERT_GROUPS
TOP_K = 2
EXPERT_FF = D_MODEL // 2
Q_BLOCK = 128
ROPE_THETA = 10000.0
EPS = 1e-6
ATTN_SCALE = HEAD_DIM ** -0.5

GQA_Q_WIDTH = GQA_HEADS * HEAD_DIM
GQA_KV_WIDTH = GQA_KV_HEADS * HEAD_DIM
NA_WIDTH = NA_HEADS * HEAD_DIM
PROJ_SIZES = (GQA_Q_WIDTH, GQA_KV_WIDTH, GQA_KV_WIDTH, NA_WIDTH, NA_WIDTH, NA_WIDTH,
              FOURIER_WIDTH, 2 * CONV_WIDTH, N_BRANCHES * D_MODEL)
PROJ_SPLITS = tuple(int(v) for v in np.cumsum(PROJ_SIZES)[:-1])
IN_WIDTH = sum(PROJ_SIZES)

kernel_name = "hybrid_parallel_mixer_moe_dit"


def rms_norm(x, g):
    xf = x.astype(jnp.float32)
    y = xf * lax.rsqrt(jnp.mean(xf * xf, axis=-1, keepdims=True) + EPS)
    return (y * g.astype(jnp.float32)).astype(x.dtype)


def axial_rope(n_tokens):
    t = jnp.arange(n_tokens, dtype=jnp.int32)
    row = (t // GRID_W).astype(jnp.float32)
    col = (t % GRID_W).astype(jnp.float32)
    n_pairs = HEAD_DIM // 4
    inv_freq = ROPE_THETA ** (-jnp.arange(n_pairs, dtype=jnp.float32) / n_pairs)
    ang = jnp.concatenate([row[:, None] * inv_freq, col[:, None] * inv_freq], axis=-1)
    return jnp.cos(ang), jnp.sin(ang)


def apply_rope(x, cos, sin):
    xf = x.astype(jnp.float32).reshape(*x.shape[:-1], HEAD_DIM // 2, 2)
    x1, x2 = xf[..., 0], xf[..., 1]
    c = cos[None, :, None, :]
    s = sin[None, :, None, :]
    return jnp.stack([x1 * c - x2 * s, x1 * s + x2 * c], axis=-1).reshape(x.shape).astype(x.dtype)


def attend(qg, k, v):
    s = jnp.einsum('bqhgd,bkhd->bhgqk', qg, k, preferred_element_type=jnp.float32) * ATTN_SCALE
    p = jax.nn.softmax(s, axis=-1).astype(v.dtype)
    return jnp.einsum('bhgqk,bkhd->bqhgd', p, v)


def dense_attention(q, k, v):
    B, L, hq, _ = q.shape
    hkv = k.shape[2]
    out = attend(q.reshape(B, L, hkv, hq // hkv, HEAD_DIM), k, v)
    return out.reshape(B, L, hq * HEAD_DIM)


def gqa_latent(q, k, v, k_ctx, v_ctx):
    B, S = q.shape[:2]
    g = GQA_HEADS // GQA_KV_HEADS
    k_all = jnp.concatenate([k, k_ctx], axis=1)
    v_all = jnp.concatenate([v, v_ctx], axis=1)
    nb = S // Q_BLOCK
    qb = q.reshape(B, nb, Q_BLOCK, GQA_KV_HEADS, g, HEAD_DIM).transpose(1, 0, 2, 3, 4, 5)
    out = lax.map(lambda qblk: attend(qblk, k_all, v_all), qb)
    return out.transpose(1, 0, 2, 3, 4, 5).reshape(B, S, GQA_Q_WIDTH)


def neighbourhood_latent(q, k, v, k_ctx, v_ctx, rpb):
    B, S = q.shape[:2]
    rows = S // GRID_W
    wr = min(NA_WIN_ROWS, rows)
    qg = q.reshape(B, rows, GRID_W, NA_HEADS, HEAD_DIM)
    kg = k.reshape(B, rows, GRID_W, NA_HEADS, HEAD_DIM)
    vg = v.reshape(B, rows, GRID_W, NA_HEADS, HEAD_DIM)
    cols = np.arange(GRID_W)
    col_start = np.clip(cols - NA_WIN_COLS // 2, 0, GRID_W - NA_WIN_COLS)
    col_idx = col_start[:, None] + np.arange(NA_WIN_COLS)[None, :]
    col_bias_idx = col_idx - cols[:, None] + (NA_WIN_COLS - 1)
    row_starts = jnp.clip(jnp.arange(rows) - NA_WIN_ROWS // 2, 0, rows - wr)
    rpb_f = rpb.astype(jnp.float32)
    n_win = wr * NA_WIN_COLS

    def row_block(args):
        r, r0, q_row = args
        k_band = lax.dynamic_slice_in_dim(kg, r0, wr, axis=1)
        v_band = lax.dynamic_slice_in_dim(vg, r0, wr, axis=1)
        k_win = k_band[:, :, col_idx]
        v_win = v_band[:, :, col_idx]
        s_win = jnp.einsum('bqhd,bwqjhd->bhqwj', q_row, k_win,
                           preferred_element_type=jnp.float32) * ATTN_SCALE
        row_bias_idx = r0 + jnp.arange(wr) - r + (NA_WIN_ROWS - 1)
        bias = rpb_f[:, row_bias_idx][:, :, col_bias_idx]
        s_win = s_win + bias.transpose(0, 2, 1, 3)[None]
        s_ctx = jnp.einsum('bqhd,bkhd->bhqk', q_row, k_ctx,
                           preferred_element_type=jnp.float32) * ATTN_SCALE
        s = jnp.concatenate([s_win.reshape(B, NA_HEADS, GRID_W, n_win), s_ctx], axis=-1)
        p = jax.nn.softmax(s, axis=-1).astype(v.dtype)
        p_win = p[..., :n_win].reshape(B, NA_HEADS, GRID_W, wr, NA_WIN_COLS)
        p_ctx = p[..., n_win:]
        return (jnp.einsum('bhqwj,bwqjhd->bqhd', p_win, v_win)
                + jnp.einsum('bhqk,bkhd->bqhd', p_ctx, v_ctx))

    out = lax.map(row_block, (jnp.arange(rows), row_starts, qg.transpose(1, 0, 2, 3, 4)))
    return out.transpose(1, 0, 2, 3, 4).reshape(B, S, NA_WIDTH)


def fourier_mix(u):
    B, L = u.shape[:2]
    ug = u.astype(jnp.float32).reshape(B, L, FOURIER_GROUPS, FOURIER_GROUP_DIM)
    y = jnp.fft.fft2(ug, axes=(1, 3), norm='ortho').real
    return y.reshape(B, L, FOURIER_WIDTH).astype(u.dtype)


def conformer_conv(u, conv_w, conv_b, conv_g):
    a, b = jnp.split(u, 2, axis=-1)
    z = a * jax.nn.sigmoid(b)
    z = lax.conv_general_dilated(z, conv_w[:, None, :], window_strides=(1,),
                                 padding=[(CONV_KERNEL // 2, CONV_KERNEL // 2)],
                                 dimension_numbers=('NWC', 'WIO', 'NWC'),
                                 feature_group_count=CONV_WIDTH) + conv_b
    return jax.nn.silu(rms_norm(z, conv_g))


def merge_branches(branches, gate_logits, w_branch, w_out):
    y = jnp.stack(branches, axis=-2)
    z = jnp.einsum('blnc,ncd->blnd', y, w_branch)
    g = jax.nn.sigmoid(gate_logits.reshape(*gate_logits.shape[:-1], N_BRANCHES, D_MODEL)
                       .astype(jnp.float32)).astype(z.dtype)
    return jnp.sum(g * z, axis=-2) @ w_out


def token_mixer(h, hc, w_in, q_g, k_g, rpb, conv_w, conv_b, conv_g, w_branch, w_out, cos, sin, update_ctx):
    B, S, _ = h.shape
    C = hc.shape[1]
    gq, gk, gv, nq, nk, nv, fu, cu, gl = jnp.split(h @ w_in, PROJ_SPLITS, axis=-1)
    cgq, cgk, cgv, cnq, cnk, cnv, cfu, ccu, cgl = jnp.split(hc @ w_in, PROJ_SPLITS, axis=-1)
    q = apply_rope(rms_norm(gq.reshape(B, S, GQA_HEADS, HEAD_DIM), q_g), cos, sin)
    k = apply_rope(rms_norm(gk.reshape(B, S, GQA_KV_HEADS, HEAD_DIM), k_g), cos, sin)
    v = gv.reshape(B, S, GQA_KV_HEADS, HEAD_DIM)
    kc = rms_norm(cgk.reshape(B, C, GQA_KV_HEADS, HEAD_DIM), k_g)
    vc = cgv.reshape(B, C, GQA_KV_HEADS, HEAD_DIM)
    y_gqa = gqa_latent(q, k, v, kc, vc)
    nkc = cnk.reshape(B, C, NA_HEADS, HEAD_DIM)
    nvc = cnv.reshape(B, C, NA_HEADS, HEAD_DIM)
    y_na = neighbourhood_latent(nq.reshape(B, S, NA_HEADS, HEAD_DIM), nk.reshape(B, S, NA_HEADS, HEAD_DIM),
                                nv.reshape(B, S, NA_HEADS, HEAD_DIM), nkc, nvc, rpb)
    y_fn = fourier_mix(fu)
    y_cv = conformer_conv(cu, conv_w, conv_b, conv_g)
    out = merge_branches((y_gqa, y_na, y_fn, y_cv), gl, w_branch, w_out)
    if not update_ctx:
        return out, None
    qc = rms_norm(cgq.reshape(B, C, GQA_HEADS, HEAD_DIM), q_g)
    yc_gqa = dense_attention(qc, kc, vc)
    yc_na = dense_attention(cnq.reshape(B, C, NA_HEADS, HEAD_DIM), nkc, nvc)
    yc_fn = fourier_mix(cfu)
    yc_cv = conformer_conv(ccu, conv_w, conv_b, conv_g)
    out_c = merge_branches((yc_gqa, yc_na, yc_fn, yc_cv), cgl, w_branch, w_out)
    return out, out_c


def moe(h, w_router, router_bias, w_gu, w_down):
    lead = h.shape[:-1]
    t = h.reshape(-1, D_MODEL)
    scores = jax.nn.sigmoid((t @ w_router).astype(jnp.float32))
    sel = scores + router_bias.astype(jnp.float32)
    group_score = lax.top_k(sel.reshape(-1, N_EXPERT_GROUPS, EXPERTS_PER_GROUP), TOP_K)[0].sum(-1)
    g_idx = jnp.argmax(group_score, axis=-1)
    in_group = (jnp.arange(N_EXPERTS) // EXPERTS_PER_GROUP)[None, :] == g_idx[:, None]
    _, top_idx = lax.top_k(jnp.where(in_group, sel, -jnp.inf), TOP_K)
    w = jnp.take_along_axis(scores, top_idx, axis=-1)
    w = w / jnp.sum(w, axis=-1, keepdims=True)
    gate = jnp.sum(jax.nn.one_hot(top_idx, N_EXPERTS, dtype=jnp.float32) * w[..., None], axis=1).astype(t.dtype)
    out = jnp.zeros_like(t)
    for e in range(N_EXPERTS):
        a, b = jnp.split(t @ w_gu[e], 2, axis=-1)
        out = out + gate[:, e:e + 1] * ((jax.nn.silu(a) * b) @ w_down[e])
    return out.reshape(*lead, D_MODEL)


def setup_inputs(seed: int = 0) -> dict:
    key = jax.random.key(seed)
    ks = jax.random.split(key, 22)
    D = D_MODEL

    def nrm(k, shape, std):
        return jax.random.normal(k, shape, jnp.float32) * std

    return {
        "x": nrm(ks[0], (BATCH, SEQ, D), 1.0),
        "c": nrm(ks[1], (BATCH, D), 1.0),
        "ctx": nrm(ks[2], (BATCH, CTX_LEN, D), 1.0),
        "c_ctx": nrm(ks[3], (D,), 1.0),
        "w_mod": nrm(ks[4], (DEPTH, D, 6 * D), 0.5 * D ** -0.5),
        "b_mod": nrm(ks[5], (DEPTH, 6 * D), 0.02),
        "norm1_g": 1.0 + nrm(ks[6], (DEPTH, D), 0.02),
        "norm2_g": 1.0 + nrm(ks[7], (DEPTH, D), 0.02),
        "w_in": nrm(ks[8], (DEPTH, D, IN_WIDTH), D ** -0.5),
        "q_norm_g": 1.0 + nrm(ks[9], (DEPTH, HEAD_DIM), 0.02),
        "k_norm_g": 1.0 + nrm(ks[10], (DEPTH, HEAD_DIM), 0.02),
        "na_rpb": nrm(ks[11], (DEPTH, NA_HEADS, 2 * NA_WIN_ROWS - 1, 2 * NA_WIN_COLS - 1), 0.02),
        "conv_w": nrm(ks[12], (DEPTH, CONV_KERNEL, CONV_WIDTH), CONV_KERNEL ** -0.5),
        "conv_b": nrm(ks[13], (DEPTH, CONV_WIDTH), 0.01),
        "conv_norm_g": 1.0 + nrm(ks[14], (DEPTH, CONV_WIDTH), 0.02),
        "w_branch": nrm(ks[15], (DEPTH, N_BRANCHES, MIX_WIDTH, D), MIX_WIDTH ** -0.5),
        "w_out": nrm(ks[16], (DEPTH, D, D), D ** -0.5),
        "w_router": nrm(ks[17], (D, N_EXPERTS), D ** -0.5),
        "router_bias": nrm(ks[18], (N_EXPERTS,), 0.01),
        "w_expert_gu": nrm(ks[19], (DEPTH, N_EXPERTS, D, 2 * EXPERT_FF), D ** -0.5),
        "w_expert_down": nrm(ks[20], (DEPTH, N_EXPERTS, EXPERT_FF, D), EXPERT_FF ** -0.5),
        "final_norm_g": 1.0 + nrm(ks[21], (D,), 0.02),
    }


def reference(x, c, ctx, c_ctx, w_mod, b_mod, norm1_g, norm2_g, w_in, q_norm_g, k_norm_g, na_rpb,
              conv_w, conv_b, conv_norm_g, w_branch, w_out, w_router, router_bias,
              w_expert_gu, w_expert_down, final_norm_g):
    S = x.shape[1]
    cos, sin = axial_rope(S)
    xc = ctx
    for l in range(DEPTH):
        update_ctx = l < DEPTH - 1
        m = jax.nn.silu(c) @ w_mod[l] + b_mod[l]
        sh1, sc1, g1, sh2, sc2, g2 = [t[:, None, :] for t in jnp.split(m, 6, axis=-1)]
        mc = jax.nn.silu(c_ctx) @ w_mod[l] + b_mod[l]
        csh1, csc1, cg1, csh2, csc2, cg2 = jnp.split(mc, 6, axis=-1)
        h = rms_norm(x, norm1_g[l]) * (1.0 + sc1) + sh1
        hc = rms_norm(xc, norm1_g[l]) * (1.0 + csc1) + csh1
        y, yc = token_mixer(h, hc, w_in[l], q_norm_g[l], k_norm_g[l], na_rpb[l], conv_w[l], conv_b[l],
                            conv_norm_g[l], w_branch[l], w_out[l], cos, sin, update_ctx)
        x = x + g1 * y
        h2 = rms_norm(x, norm2_g[l]) * (1.0 + sc2) + sh2
        if update_ctx:
            xc = xc + cg1 * yc
            h2c = rms_norm(xc, norm2_g[l]) * (1.0 + csc2) + csh2
            z = moe(jnp.concatenate([h2, h2c], axis=1), w_router, router_bias, w_expert_gu[l], w_expert_down[l])
            x = x + g2 * z[:, :S]
            xc = xc + cg2 * z[:, S:]
        else:
            x = x + g2 * moe(h2, w_router, router_bias, w_expert_gu[l], w_expert_down[l])
    return rms_norm(x, final_norm_g)
```

```python
import functools

import numpy as np
import jax
import jax.numpy as jnp
from jax import lax
from jax.experimental import pallas as pl
from jax.experimental.pallas import tpu as pltpu

D_MODEL = 1024
GRID_W = 64
MIX_WIDTH = 512
HEAD_DIM = 64
N_HEADS = 8
GQA_KV_HEADS = 2
NA_WIN_ROWS = 8
NA_WIN_COLS = 16
FOURIER_GROUP_DIM = 128
CONV_KERNEL = 31
N_BRANCHES = 4
N_EXPERTS = 16
EXPERTS_PER_GROUP = 4
EXPERT_FF = 512
ROPE_THETA = 10000.0
EPS = 1e-6
ATTN_SCALE = HEAD_DIM ** -0.5

_OFF_GQ, _OFF_GK, _OFF_GV = 0, 512, 640
_OFF_NQ, _OFF_NK, _OFF_NV = 768, 1280, 1792
_OFF_FU, _OFF_CU, _OFF_GL = 2304, 2816, 3840
IN_WIDTH = 7936

ROW_TILE = 256
NA_BLOCK_ROWS = ROW_TILE // GRID_W
NA_BAND_ROWS = NA_BLOCK_ROWS + NA_WIN_ROWS
CONV_HALO = 16
MASK_VALUE = -1e30
VMEM_LIMIT = 56 * 1024 * 1024

F32 = jnp.float32
BF16 = jnp.bfloat16


def _params(sem, vmem=VMEM_LIMIT):
    return pltpu.CompilerParams(dimension_semantics=sem, vmem_limit_bytes=vmem)


def _const_spec(shape):
    nd = len(shape)
    return pl.BlockSpec(shape, lambda *_: (0,) * nd, pipeline_mode=pl.Buffered(1))


def _sigmoid(v):
    return 1.0 / (1.0 + jnp.exp(-v))


def _dot(a, b):
    return jnp.dot(a, b, preferred_element_type=F32)


def _dot_t(a, b):
    return lax.dot_general(a, b, (((1,), (1,)), ((), ())), preferred_element_type=F32)


def _mod_kernel(c_ref, w_ref, b_ref, o_ref):
    cc = c_ref[...]
    a = cc * _sigmoid(cc)
    o_ref[...] = jnp.dot(a, w_ref[...], preferred_element_type=F32,
                         precision=lax.Precision.HIGHEST) + b_ref[...]


def _modulation(c_all, w_mod, b_mod):
    depth, d, n = w_mod.shape
    tn = 1536
    return pl.pallas_call(
        _mod_kernel,
        grid=(depth, n // tn),
        in_specs=[pl.BlockSpec((8, d), lambda l, j: (0, 0)),
                  pl.BlockSpec((None, d, tn), lambda l, j: (l, 0, j)),
                  pl.BlockSpec((None, 1, tn), lambda l, j: (l, 0, j))],
        out_specs=pl.BlockSpec((None, 8, tn), lambda l, j: (l, 0, j)),
        out_shape=jax.ShapeDtypeStruct((depth, 8, n), F32),
        compiler_params=_params(("arbitrary", "arbitrary")),
        name="modulation",
    )(c_all, w_mod, b_mod.reshape(depth, 1, n))


def _modulated_norm(x, g, shift, scale):
    ms = jnp.mean(x * x, axis=-1, keepdims=True)
    return x * lax.rsqrt(ms + EPS) * g * (1.0 + scale) + shift


def _head_norm(q, gsum, gain):
    sq = q * q
    hi = sq.astype(BF16)
    lo = (sq - hi.astype(F32)).astype(BF16)
    ss = (_dot(hi, gsum) + _dot(lo, gsum)) * (1.0 / HEAD_DIM)
    return q * lax.rsqrt(ss + EPS) * gain


def _rope(q, cos, sin_signed):
    n = q.shape[-1]
    lane = lax.broadcasted_iota(jnp.int32, q.shape, 1)
    swapped = jnp.where(lane % 2 == 0, pltpu.roll(q, n - 1, 1), pltpu.roll(q, 1, 1))
    return q * cos + swapped * sin_signed


def _inproj_kernel(x_ref, mod_ref, g_ref, w_ref, cq_ref, sq_ref, ck_ref, sk_ref, qg_ref, kg_ref,
                   gsum_ref, dcs_ref,
                   q_ref, k_ref, v_ref, nq_ref, nk_ref, nv_ref, fab_ref, cu_ref, gl_ref):
    mod = mod_ref[...]
    h = _modulated_norm(x_ref[...], g_ref[...], mod[0:1], mod[1:2]).astype(BF16)

    def seg(off, width):
        return _dot(h, w_ref[:, off:off + width])

    gsum = gsum_ref[...]
    q = _head_norm(seg(_OFF_GQ, 512), gsum, qg_ref[...])
    q_ref[...] = _rope(q, cq_ref[...], sq_ref[...]).astype(BF16)
    kv = seg(_OFF_GK, 256)
    k = _head_norm(kv[:, :128], gsum[:128, :128], kg_ref[...])
    k_ref[...] = _rope(k, ck_ref[...], sk_ref[...]).astype(BF16)
    v_ref[...] = kv[:, 128:].astype(BF16)
    nq_ref[...] = (seg(_OFF_NQ, 512) * ATTN_SCALE).astype(BF16)
    nk_ref[...] = seg(_OFF_NK, 512).astype(BF16)
    nv_ref[...] = seg(_OFF_NV, 512).astype(BF16)
    ab = _dot(seg(_OFF_FU, 512).astype(BF16), dcs_ref[...])
    fab_ref[0] = ab[:, :512].astype(BF16)
    fab_ref[1] = ab[:, 512:].astype(BF16)
    for j in range(2):
        cu_ref[:, j * 512:(j + 1) * 512] = seg(_OFF_CU + j * 512, 512).astype(BF16)
    for j in range(4):
        gl_ref[:, j * 1024:(j + 1) * 1024] = seg(_OFF_GL + j * 1024, 1024).astype(BF16)


def _inproj(xa, modtab, g1, w_in, tabs, consts, n_lat):
    b, nt, d = xa.shape
    tm = ROW_TILE
    cq, sq, ck, sk = tabs
    qg, kg, gsum, dcs = consts
    row = lambda width: pl.BlockSpec((None, tm, width), lambda bb, i: (bb, i, 0))
    tab = lambda width: pl.BlockSpec((tm, width), lambda bb, i: (i, 0))
    out_shapes = [jax.ShapeDtypeStruct((b, nt, wd), BF16) for wd in (512, 128, 128, 512, 512, 512)]
    out_shapes += [jax.ShapeDtypeStruct((b, 2, nt, 512), BF16),
                   jax.ShapeDtypeStruct((b, nt, 1024), BF16),
                   jax.ShapeDtypeStruct((b, nt, 4096), BF16)]
    out_specs = [row(512), row(128), row(128), row(512), row(512), row(512),
                 pl.BlockSpec((None, 2, tm, 512), lambda bb, i: (bb, 0, i, 0)),
                 row(1024), row(4096)]
    return pl.pallas_call(
        _inproj_kernel,
        grid=(b, nt // tm),
        in_specs=[row(d),
                  pl.BlockSpec((None, None, 6, d), lambda bb, i: (bb, i // n_lat, 0, 0)),
                  _const_spec((1, d)),
                  _const_spec((d, IN_WIDTH)),
                  tab(512), tab(512), tab(128), tab(128),
                  _const_spec((1, 512)), _const_spec((1, 128)),
                  _const_spec((512, 512)), _const_spec((512, 1024))],
        out_specs=out_specs,
        out_shape=out_shapes,
        compiler_params=_params(("parallel", "parallel")),
        name="inproj",
    )(xa, modtab, g1, w_in, cq, sq, ck, sk, qg, kg, gsum, dcs)


def _softmax_pv(parts):
    m = None
    for s, _ in parts:
        mi = jnp.max(s, axis=-1, keepdims=True)
        m = mi if m is None else jnp.maximum(m, mi)
    l = None
    o = None
    for s, v in parts:
        p = jnp.exp(s - m)
        li = jnp.sum(p, axis=-1, keepdims=True)
        oi = _dot(p.astype(BF16), v)
        l = li if l is None else l + li
        o = oi if o is None else o + oi
    return o / l


def _gqa_kernel(q_ref, k_ref, v_ref, o_ref, *, seq, ctx, n_lat):
    i = pl.program_id(1)
    group = N_HEADS // GQA_KV_HEADS

    def attend(lo, n):
        for h in range(N_HEADS):
            c0 = (h // group) * HEAD_DIM
            qh = q_ref[:, h * HEAD_DIM:(h + 1) * HEAD_DIM]
            kk = k_ref[lo:lo + n, c0:c0 + HEAD_DIM]
            vv = v_ref[lo:lo + n, c0:c0 + HEAD_DIM]
            o = _softmax_pv([(_dot_t(qh, kk), vv)])
            o_ref[:, h * HEAD_DIM:(h + 1) * HEAD_DIM] = o.astype(o_ref.dtype)

    @pl.when(i < n_lat)
    def _():
        attend(0, seq + ctx)

    @pl.when(i >= n_lat)
    def _():
        attend(seq, ctx)


def _gqa(q, k, v, seq, ctx, n_tiles):
    b, nt, _ = q.shape
    tm = ROW_TILE
    n_lat = seq // tm
    return pl.pallas_call(
        functools.partial(_gqa_kernel, seq=seq, ctx=ctx, n_lat=n_lat),
        grid=(b, n_tiles),
        in_specs=[pl.BlockSpec((None, tm, 512), lambda bb, i: (bb, i, 0)),
                  pl.BlockSpec((None, nt, 128), lambda bb, i: (bb, 0, 0)),
                  pl.BlockSpec((None, nt, 128), lambda bb, i: (bb, 0, 0))],
        out_specs=pl.BlockSpec((None, tm, 512), lambda bb, i: (bb, i, 0)),
        out_shape=jax.ShapeDtypeStruct((b, nt, 512), BF16),
        compiler_params=_params(("parallel", "parallel")),
        name="gqa_attention",
    )(q, k, v)


def _na_kernel(q_ref, k_ref, v_ref, bias_ref, o_ref, *, seq, ctx, n_lat):
    i = pl.program_id(1)
    band = NA_BAND_ROWS * GRID_W
    rows = seq // GRID_W

    @pl.when(i < n_lat)
    def _():
        band_row = jnp.clip(i * NA_BLOCK_ROWS - NA_WIN_ROWS // 2, 0, rows - NA_BAND_ROWS)
        start = pl.multiple_of(band_row * GRID_W, ROW_TILE)
        for h in range(N_HEADS):
            hs = slice(h * HEAD_DIM, (h + 1) * HEAD_DIM)
            qh = q_ref[:, hs]
            s_band = _dot_t(qh, k_ref[pl.ds(start, band), hs]) + bias_ref[h]
            s_ctx = _dot_t(qh, k_ref[seq:seq + ctx, hs])
            o = _softmax_pv([(s_band, v_ref[pl.ds(start, band), hs]), (s_ctx, v_ref[seq:seq + ctx, hs])])
            o_ref[:, hs] = o.astype(o_ref.dtype)

    @pl.when(i >= n_lat)
    def _():
        for h in range(N_HEADS):
            hs = slice(h * HEAD_DIM, (h + 1) * HEAD_DIM)
            o = _softmax_pv([(_dot_t(q_ref[:, hs], k_ref[seq:seq + ctx, hs]), v_ref[seq:seq + ctx, hs])])
            o_ref[:, hs] = o.astype(o_ref.dtype)


def _na(q, k, v, bias, seq, ctx, n_tiles):
    b, nt, _ = q.shape
    tm = ROW_TILE
    n_lat = seq // tm
    band = NA_BAND_ROWS * GRID_W

    def bias_idx(bb, i):
        return (jnp.where(i == 0, 0, jnp.where(i >= n_lat - 1, 2, 1)), 0, 0, 0)

    return pl.pallas_call(
        functools.partial(_na_kernel, seq=seq, ctx=ctx, n_lat=n_lat),
        grid=(b, n_tiles),
        in_specs=[pl.BlockSpec((None, tm, 512), lambda bb, i: (bb, i, 0)),
                  pl.BlockSpec((None, nt, 512), lambda bb, i: (bb, 0, 0)),
                  pl.BlockSpec((None, nt, 512), lambda bb, i: (bb, 0, 0)),
                  pl.BlockSpec((None, N_HEADS, tm, band), bias_idx)],
        out_specs=pl.BlockSpec((None, tm, 512), lambda bb, i: (bb, i, 0)),
        out_shape=jax.ShapeDtypeStruct((b, nt, 512), BF16),
        compiler_params=_params(("parallel", "parallel")),
        name="neighbourhood_attention",
    )(q, k, v, bias)


def _bias_rows_kernel(rpb_ref, onehot_ref, mask_ref, o_ref):
    o_ref[...] = jnp.dot(rpb_ref[...], onehot_ref[...], preferred_element_type=F32,
                         precision=lax.Precision.HIGHEST) + mask_ref[...]


def _na_bias(rpb, seq):
    n_dr, n_dc = 2 * NA_WIN_ROWS - 1, 2 * NA_WIN_COLS - 1
    qc = np.arange(GRID_W)
    c0 = np.clip(qc - NA_WIN_COLS // 2, 0, GRID_W - NA_WIN_COLS)
    col_ok = (qc[None, :] >= c0[:, None]) & (qc[None, :] < c0[:, None] + NA_WIN_COLS)
    col_idx = qc[None, :] - qc[:, None] + NA_WIN_COLS - 1
    onehot = np.zeros((128, GRID_W * GRID_W), np.float32)
    for j in range(n_dc):
        onehot[j] = ((col_idx == j) & col_ok).reshape(-1)
    mask = np.where(col_ok, 0.0, MASK_VALUE).astype(np.float32).reshape(1, -1)
    rpb_rows = jnp.zeros((128, 128), F32).at[:N_HEADS * n_dr, :n_dc].set(rpb.astype(F32).reshape(-1, n_dc))
    blocks = pl.pallas_call(
        _bias_rows_kernel,
        out_shape=jax.ShapeDtypeStruct((128, GRID_W * GRID_W), F32),
        name="na_bias_rows",
    )(rpb_rows, jnp.asarray(onehot), jnp.asarray(mask))
    blocks = blocks[:N_HEADS * n_dr].reshape(N_HEADS, n_dr, GRID_W, GRID_W)
    masked = jnp.full((N_HEADS, GRID_W, GRID_W), MASK_VALUE, F32)

    rows = seq // GRID_W
    n_blk = rows // NA_BLOCK_ROWS
    tables = []
    for blk in (0, 1, n_blk - 1):
        band_row = int(np.clip(blk * NA_BLOCK_ROWS - NA_WIN_ROWS // 2, 0, rows - NA_BAND_ROWS))
        strips = []
        for qr in range(blk * NA_BLOCK_ROWS, (blk + 1) * NA_BLOCK_ROWS):
            r0 = int(np.clip(qr - NA_WIN_ROWS // 2, 0, rows - NA_WIN_ROWS))
            strip = [blocks[:, kr - qr + NA_WIN_ROWS - 1] if r0 <= kr < r0 + NA_WIN_ROWS else masked
                     for kr in range(band_row, band_row + NA_BAND_ROWS)]
            strips.append(jnp.concatenate(strip, axis=-1))
        tables.append(jnp.concatenate(strips, axis=1))
    return jnp.stack(tables)


def _dft_kernel(wc_ref, ws_ref, a_ref, b_ref, o_ref):
    y = _dot(wc_ref[...], a_ref[...]) - _dot(ws_ref[...], b_ref[...])
    o_ref[...] = y.astype(o_ref.dtype)


def _dft_mats(n):
    r = int(round(np.sqrt(n)))
    assert r * r == n
    k = jnp.arange(n, dtype=jnp.int32)[:, None]
    j = jnp.arange(r, dtype=jnp.int32)[None, :]
    a = ((k * j) % r).astype(F32) * (2.0 * np.pi / r)
    bb = ((k * j) % n).astype(F32) * (2.0 * np.pi / n)
    ca, sa = jnp.cos(a)[:, :, None], jnp.sin(a)[:, :, None]
    cb, sb = jnp.cos(bb)[:, None, :], jnp.sin(bb)[:, None, :]
    scale = 1.0 / np.sqrt(n * FOURIER_GROUP_DIM)
    cos = ((ca * cb - sa * sb) * scale).reshape(n, n)
    sin = ((sa * cb + ca * sb) * scale).reshape(n, n)
    return cos.astype(BF16), sin.astype(BF16)


def _fourier_latent(fab, seq, mats):
    b, _, nt, w = fab.shape
    tf = 512 if seq % 512 == 0 else ROW_TILE
    wc, ws = mats
    return pl.pallas_call(
        _dft_kernel,
        grid=(seq // tf, b),
        in_specs=[pl.BlockSpec((tf, seq), lambda i, bb: (i, 0)),
                  pl.BlockSpec((tf, seq), lambda i, bb: (i, 0)),
                  pl.BlockSpec((None, None, seq, w), lambda i, bb: (bb, 0, 0, 0)),
                  pl.BlockSpec((None, None, seq, w), lambda i, bb: (bb, 1, 0, 0))],
        out_specs=pl.BlockSpec((None, tf, w), lambda i, bb: (bb, i, 0)),
        out_shape=jax.ShapeDtypeStruct((b, nt, w), BF16),
        compiler_params=_params(("parallel", "parallel")),
        name="fourier_latent",
    )(wc, ws, fab, fab)


def _dft_ctx_kernel(wc_ref, ws_ref, a_ref, b_ref, y_in_ref, o_ref):
    del y_in_ref
    _dft_kernel(wc_ref, ws_ref, a_ref, b_ref, o_ref)


def _fourier_ctx(fab, y_fn, seq, ctx, mats):
    b, _, nt, w = fab.shape
    blk = seq // ctx
    wc, ws = mats
    return pl.pallas_call(
        _dft_ctx_kernel,
        grid=(b,),
        in_specs=[_const_spec((ctx, ctx)), _const_spec((ctx, ctx)),
                  pl.BlockSpec((None, None, ctx, w), lambda bb: (bb, 0, blk, 0)),
                  pl.BlockSpec((None, None, ctx, w), lambda bb: (bb, 1, blk, 0)),
                  pl.BlockSpec(memory_space=pl.ANY)],
        out_specs=pl.BlockSpec((None, ctx, w), lambda bb: (bb, blk, 0)),
        out_shape=jax.ShapeDtypeStruct(y_fn.shape, y_fn.dtype),
        input_output_aliases={4: 0},
        compiler_params=_params(("parallel",)),
        name="fourier_ctx",
    )(wc, ws, fab, fab, y_fn)


def _conv_kernel(cu_ref, w_ref, b_ref, g_ref, o_ref, z_ref, *, n_lat, n_tiles):
    tm = ROW_TILE
    halo = CONV_HALO
    zeros = jnp.zeros((halo, MIX_WIDTH), F32)
    lat_end = halo + n_lat * tm
    z_ref[0:halo, :] = zeros
    z_ref[lat_end:lat_end + halo, :] = zeros
    ctx_end = lat_end + halo + (n_tiles - n_lat) * tm
    z_ref[ctx_end:ctx_end + halo, :] = zeros

    def tile_base(t):
        return pl.multiple_of(t * tm + halo + jnp.where(t >= n_lat, halo, 0), 8)

    def glu(t, carry):
        r = pl.multiple_of(t * tm, tm)
        a = cu_ref[pl.ds(r, tm), 0:MIX_WIDTH].astype(F32)
        g = cu_ref[pl.ds(r, tm), MIX_WIDTH:2 * MIX_WIDTH].astype(F32)
        z_ref[pl.ds(tile_base(t), tm), :] = a * _sigmoid(g)
        return carry

    lax.fori_loop(0, n_tiles, glu, 0)

    def conv(t, carry):
        start = pl.multiple_of(tile_base(t) - halo, 8)
        cols = []
        for c in range(MIX_WIDTH // 128):
            cs = slice(c * 128, (c + 1) * 128)
            win = z_ref[pl.ds(start, tm + 2 * halo), cs]
            acc = jnp.zeros((tm, 128), F32)
            for j in range(CONV_KERNEL):
                off = halo - CONV_KERNEL // 2 + j
                acc = acc + win[off:off + tm, :] * w_ref[j:j + 1, cs]
            cols.append(acc)
        y = jnp.concatenate(cols, axis=1) + b_ref[...]
        ms = jnp.mean(y * y, axis=-1, keepdims=True)
        y = y * lax.rsqrt(ms + EPS) * g_ref[...]
        r = pl.multiple_of(t * tm, tm)
        o_ref[pl.ds(r, tm), :] = (y * _sigmoid(y)).astype(o_ref.dtype)
        return carry

    lax.fori_loop(0, n_tiles, conv, 0)


def _conv(cu, conv_w, conv_b, conv_g, seq, n_tiles):
    b, nt, _ = cu.shape
    n_lat = seq // ROW_TILE
    return pl.pallas_call(
        functools.partial(_conv_kernel, n_lat=n_lat, n_tiles=n_tiles),
        grid=(b,),
        in_specs=[pl.BlockSpec((None, nt, 2 * MIX_WIDTH), lambda bb: (bb, 0, 0)),
                  _const_spec((CONV_KERNEL, MIX_WIDTH)),
                  _const_spec((1, MIX_WIDTH)), _const_spec((1, MIX_WIDTH))],
        out_specs=pl.BlockSpec((None, nt, MIX_WIDTH), lambda bb: (bb, 0, 0)),
        out_shape=jax.ShapeDtypeStruct((b, nt, MIX_WIDTH), BF16),
        scratch_shapes=[pltpu.VMEM((nt + 3 * CONV_HALO, MIX_WIDTH), F32)],
        compiler_params=_params(("parallel",)),
        name="conformer_conv",
    )(cu, conv_w, conv_b, conv_g)


def _route(logits_t, bias):
    score = _sigmoid(logits_t)
    sel = score + bias
    sel_r = [sel[e:e + 1] for e in range(N_EXPERTS)]
    n_groups = N_EXPERTS // EXPERTS_PER_GROUP

    def beats(a, ia, b, ib):
        return (a >= b) if ia < ib else (a > b)

    picked = []
    for g in range(n_groups):
        ids = range(g * EXPERTS_PER_GROUP, (g + 1) * EXPERTS_PER_GROUP)
        for e in ids:
            rank = sum(beats(sel_r[o], o, sel_r[e], e).astype(F32) for o in ids if o != e)
            picked.append(rank < 2.0)
    group_score = []
    for g in range(n_groups):
        ids = range(g * EXPERTS_PER_GROUP, (g + 1) * EXPERTS_PER_GROUP)
        group_score.append(sum(jnp.where(picked[e], sel_r[e], 0.0) for e in ids))
    gates = []
    for g in range(n_groups):
        rank = sum(beats(group_score[o], o, group_score[g], g).astype(F32) for o in range(n_groups) if o != g)
        best = rank < 1.0
        ids = range(g * EXPERTS_PER_GROUP, (g + 1) * EXPERTS_PER_GROUP)
        w = [jnp.where(picked[e] & best, score[e:e + 1], 0.0) for e in ids]
        gates.append((w, best))
    denom = sum(sum(w) for w, _ in gates)
    return jnp.concatenate([wi / denom for w, _ in gates for wi in w], axis=0)


def _merge_kernel(yg_ref, yn_ref, yf_ref, yc_ref, gl_ref, x_ref, mod_ref, g2_ref, wb_ref, wo_ref,
                  wr_ref, rb_ref, xo_ref, h2_ref, gate_ref):
    acc = None
    for n, y_ref in enumerate((yg_ref, yn_ref, yf_ref, yc_ref)):
        z = _dot(y_ref[...], wb_ref[n])
        gate = _sigmoid(gl_ref[:, n * D_MODEL:(n + 1) * D_MODEL].astype(F32))
        acc = gate * z if acc is None else acc + gate * z
    mod = mod_ref[...]
    x = x_ref[...] + mod[2:3] * _dot(acc.astype(BF16), wo_ref[...])
    xo_ref[...] = x
    h2 = _modulated_norm(x, g2_ref[...], mod[3:4], mod[4:5])
    h2_ref[...] = h2.astype(BF16)
    logits_t = lax.dot_general(wr_ref[...], h2, (((1,), (1,)), ((), ())),
                               preferred_element_type=F32, precision=lax.Precision.HIGHEST)
    gate_t = _route(logits_t, rb_ref[...])
    pad = jnp.zeros((128 - N_EXPERTS, gate_t.shape[1]), F32)
    gate_ref[...] = jnp.concatenate([gate_t, pad], axis=0).T


def _merge(ys, gl, xa, modtab, g2, w_branch, w_out, w_router_t, router_bias, n_lat, n_tiles):
    b, nt, d = xa.shape
    tm = ROW_TILE
    row = lambda width: pl.BlockSpec((None, tm, width), lambda bb, i: (bb, i, 0))
    return pl.pallas_call(
        _merge_kernel,
        grid=(b, n_tiles),
        in_specs=[row(512), row(512), row(512), row(512), row(4096), row(d),
                  pl.BlockSpec((None, None, 6, d), lambda bb, i: (bb, i // n_lat, 0, 0)),
                  _const_spec((1, d)),
                  _const_spec((N_BRANCHES, MIX_WIDTH, d)),
                  _const_spec((d, d)),
                  _const_spec((N_EXPERTS, d)),
                  _const_spec((N_EXPERTS, 1))],
        out_specs=[row(d), row(d), row(128)],
        out_shape=[jax.ShapeDtypeStruct((b, n_tiles * tm, d), F32),
                   jax.ShapeDtypeStruct((b, n_tiles * tm, d), BF16),
                   jax.ShapeDtypeStruct((b, n_tiles * tm, 128), F32)],
        compiler_params=_params(("parallel", "parallel")),
        name="merge_router",
    )(*ys, gl, xa, modtab, g2, w_branch, w_out, w_router_t, router_bias)


def _moe_kernel(h_ref, gate_ref, wgu_ref, wd_ref, o_ref, acc_ref):
    e = pl.program_id(1)
    ab = _dot(h_ref[...], wgu_ref[...])
    a = ab[:, :EXPERT_FF]
    hid = (a * _sigmoid(a) * ab[:, EXPERT_FF:]).astype(BF16)
    y = _dot(hid, wd_ref[...])
    gates = gate_ref[...]
    lane = lax.broadcasted_iota(jnp.int32, gates.shape, 1)
    col = jnp.sum(jnp.where(lane == e, gates, 0.0), axis=-1, keepdims=True)

    @pl.when(e == 0)
    def _():
        acc_ref[...] = col * y

    @pl.when(e > 0)
    def _():
        acc_ref[...] += col * y

    @pl.when(e == N_EXPERTS - 1)
    def _():
        o_ref[...] = acc_ref[...]


def _moe(h2, gate, w_gu, w_down, tm):
    t, d = h2.shape
    return pl.pallas_call(
        _moe_kernel,
        grid=(t // tm, N_EXPERTS),
        in_specs=[pl.BlockSpec((tm, d), lambda i, e: (i, 0)),
                  pl.BlockSpec((tm, 128), lambda i, e: (i, 0)),
                  pl.BlockSpec((None, d, 2 * EXPERT_FF), lambda i, e: (e, 0, 0)),
                  pl.BlockSpec((None, EXPERT_FF, d), lambda i, e: (e, 0, 0))],
        out_specs=pl.BlockSpec((tm, d), lambda i, e: (i, 0)),
        out_shape=jax.ShapeDtypeStruct((t, d), F32),
        scratch_shapes=[pltpu.VMEM((tm, d), F32)],
        compiler_params=_params(("parallel", "arbitrary")),
        name="moe_dense",
    )(h2, gate, w_gu, w_down)


def _residual_kernel(x_ref, z_ref, mod_ref, o_ref):
    o_ref[...] = x_ref[...] + mod_ref[5:6, :] * z_ref[...]


def _final_kernel(x_ref, z_ref, mod_ref, g_ref, o_ref):
    x = x_ref[...] + mod_ref[5:6, :] * z_ref[...]
    ms = jnp.mean(x * x, axis=-1, keepdims=True)
    o_ref[...] = x * lax.rsqrt(ms + EPS) * g_ref[...]


def _residual(xa, z, modtab, n_lat, final_g=None, seq=None):
    b, nt, d = xa.shape
    tm = ROW_TILE
    row = pl.BlockSpec((None, tm, d), lambda bb, i: (bb, i, 0))
    mod = pl.BlockSpec((None, None, 6, d), lambda bb, i: (bb, i // n_lat, 0, 0))
    if final_g is None:
        return pl.pallas_call(
            _residual_kernel, grid=(b, nt // tm), in_specs=[row, row, mod], out_specs=row,
            out_shape=jax.ShapeDtypeStruct((b, nt, d), F32),
            compiler_params=_params(("parallel", "parallel")), name="moe_residual",
        )(xa, z, modtab)
    return pl.pallas_call(
        _final_kernel, grid=(b, seq // tm), in_specs=[row, row, mod, _const_spec((1, d))], out_specs=row,
        out_shape=jax.ShapeDtypeStruct((b, seq, d), F32),
        compiler_params=_params(("parallel", "parallel")), name="moe_residual_final_norm",
    )(xa, z, modtab, final_g)


def _rope_tables(seq, ctx):
    t = jnp.arange(seq, dtype=jnp.int32)
    row = (t // GRID_W).astype(F32)
    col = (t % GRID_W).astype(F32)
    n_pairs = HEAD_DIM // 4
    inv_freq = ROPE_THETA ** (-jnp.arange(n_pairs, dtype=F32) / n_pairs)
    ang = jnp.concatenate([row[:, None] * inv_freq, col[:, None] * inv_freq], axis=-1)
    cos = jnp.repeat(jnp.cos(ang), 2, axis=-1)
    sin = jnp.repeat(jnp.sin(ang), 2, axis=-1) * jnp.tile(jnp.array([-1.0, 1.0], F32), HEAD_DIM // 2)
    cos = jnp.concatenate([cos, jnp.ones((ctx, HEAD_DIM), F32)], axis=0)
    sin = jnp.concatenate([sin, jnp.zeros((ctx, HEAD_DIM), F32)], axis=0)
    ck, sk = jnp.tile(cos, (1, GQA_KV_HEADS)), jnp.tile(sin, (1, GQA_KV_HEADS))
    cq, sq = jnp.tile(cos, (1, N_HEADS)) * ATTN_SCALE, jnp.tile(sin, (1, N_HEADS)) * ATTN_SCALE
    return cq, sq, ck, sk


def _group_sum_matrix():
    idx = np.arange(MIX_WIDTH) // HEAD_DIM
    return jnp.asarray(idx[:, None] == idx[None, :], dtype=BF16)


def _channel_dft_matrix():
    c = np.arange(FOURIER_GROUP_DIM)
    ang = 2.0 * np.pi * ((c[:, None] * c[None, :]) % FOURIER_GROUP_DIM) / FOURIER_GROUP_DIM
    n_groups = MIX_WIDTH // FOURIER_GROUP_DIM
    eye = np.eye(n_groups)
    m = np.concatenate([np.kron(eye, np.cos(ang)), np.kron(eye, np.sin(ang))], axis=1)
    return jnp.asarray(m, dtype=F32).astype(BF16)


def kernel(x, c, ctx, c_ctx, w_mod, b_mod, norm1_g, norm2_g, w_in, q_norm_g, k_norm_g, na_rpb, conv_w, conv_b,
           conv_norm_g, w_branch, w_out, w_router, router_bias, w_expert_gu, w_expert_down, final_norm_g):
    b, seq, d = x.shape
    n_ctx = ctx.shape[1]
    depth = w_mod.shape[0]
    assert d == D_MODEL and seq % ROW_TILE == 0 and n_ctx == ROW_TILE and seq % n_ctx == 0
    assert seq // GRID_W >= NA_BAND_ROWS and b <= 7
    n_lat = seq // ROW_TILE
    n_all = n_lat + 1

    c_all = jnp.zeros((8, d), F32).at[:b].set(c).at[b].set(c_ctx)
    m = _modulation(c_all, w_mod, b_mod)
    m_lat = m[:, :b].reshape(depth, b, 1, 6, d)
    m_ctx = jnp.broadcast_to(m[:, b].reshape(depth, 1, 1, 6, d), (depth, b, 1, 6, d))
    modtab = jnp.concatenate([m_lat, m_ctx], axis=2)

    tabs = _rope_tables(seq, n_ctx)
    gsum = _group_sum_matrix()
    dcs = _channel_dft_matrix()
    dft_lat = _dft_mats(seq)
    dft_ctx = _dft_mats(n_ctx)
    w_router_t = w_router.T
    rb = router_bias.reshape(N_EXPERTS, 1)

    xa = jnp.concatenate([x, ctx], axis=1)
    out = None
    for l in range(depth):
        last = l == depth - 1
        n_tiles = n_lat if last else n_all
        qg = jnp.tile(q_norm_g[l], N_HEADS).reshape(1, -1)
        kg = jnp.tile(k_norm_g[l], GQA_KV_HEADS).reshape(1, -1)
        q, k, v, nq, nk, nv, fab, cu, gl = _inproj(
            xa, modtab[l], norm1_g[l].reshape(1, d), w_in[l].astype(BF16), tabs, (qg, kg, gsum, dcs), n_lat)
        y_gqa = _gqa(q, k, v, seq, n_ctx, n_tiles)
        y_na = _na(nq, nk, nv, _na_bias(na_rpb[l], seq), seq, n_ctx, n_tiles)
        y_fn = _fourier_latent(fab, seq, dft_lat)
        if not last:
            y_fn = _fourier_ctx(fab, y_fn, seq, n_ctx, dft_ctx)
        y_cv = _conv(cu, conv_w[l], conv_b[l].reshape(1, -1), conv_norm_g[l].reshape(1, -1), seq, n_tiles)
        xa, h2, gate = _merge((y_gqa, y_na, y_fn, y_cv), gl, xa, modtab[l], norm2_g[l].reshape(1, d),
                              w_branch[l].astype(BF16), w_out[l].astype(BF16), w_router_t, rb, n_lat, n_tiles)
        nt = xa.shape[1]
        tm = next(t for t in (1024, 512, 256) if (b * nt) % t == 0)
        z = _moe(h2.reshape(b * nt, d), gate.reshape(b * nt, 128),
                 w_expert_gu[l].astype(BF16), w_expert_down[l].astype(BF16), tm)
        z = z.reshape(b, nt, d)
        if last:
            out = _residual(xa, z, modtab[l], n_lat, final_norm_g.reshape(1, d), seq)
        else:
            xa = _residual(xa, z, modtab[l], n_lat)
    return out
```

```python
import functools

import numpy as np
import jax
import jax.numpy as jnp
from jax import lax
from jax.experimental import pallas as pl
from jax.experimental.pallas import tpu as pltpu

D_MODEL = 1024
GRID_W = 64
MIX_WIDTH = 512
HEAD_DIM = 64
N_HEADS = 8
GQA_KV_HEADS = 2
NA_WIN_ROWS = 8
NA_WIN_COLS = 16
FOURIER_GROUP_DIM = 128
CONV_KERNEL = 31
N_BRANCHES = 4
N_EXPERTS = 16
EXPERTS_PER_GROUP = 4
EXPERT_FF = 512
ROPE_THETA = 10000.0
EPS = 1e-6
ATTN_SCALE = HEAD_DIM ** -0.5
LOG2E = float(np.log2(np.e))
Q_SCALE = ATTN_SCALE * LOG2E

_OFF_GQ, _OFF_GK, _OFF_GV = 0, 512, 640
_OFF_NQ, _OFF_NK, _OFF_NV = 768, 1280, 1792
_OFF_FU, _OFF_CU, _OFF_GL = 2304, 2816, 3840
IN_WIDTH = 7936

ROW_TILE = 256
NA_BLOCK_ROWS = ROW_TILE // GRID_W
NA_BAND_ROWS = NA_BLOCK_ROWS + NA_WIN_ROWS
ROW_EXTRA = 128
EXPERT_TILE = 256
DISPATCH_CHUNKS = 4
CONV_HALO = 16
MASK_VALUE = -1e30
VMEM_LIMIT = 56 * 1024 * 1024

F32 = jnp.float32
BF16 = jnp.bfloat16


def _params(sem, vmem=VMEM_LIMIT):
    return pltpu.CompilerParams(dimension_semantics=sem, vmem_limit_bytes=vmem)


def _const_spec(shape):
    nd = len(shape)
    return pl.BlockSpec(shape, lambda *_: (0,) * nd, pipeline_mode=pl.Buffered(1))


def _sigmoid(v):
    return 0.5 * jnp.tanh(0.5 * v) + 0.5


def _sigmoid_exp(v):
    return 1.0 / (1.0 + jnp.exp(-v))


def _dot(a, b):
    return jnp.dot(a, b, preferred_element_type=F32)


def _dot_t(a, b):
    return lax.dot_general(a, b, (((1,), (1,)), ((), ())), preferred_element_type=F32)


def _mod_kernel(c_ref, w_ref, b_ref, o_ref):
    cc = c_ref[...]
    a = cc * _sigmoid(cc)
    o_ref[...] = jnp.dot(a, w_ref[...], preferred_element_type=F32,
                         precision=lax.Precision.HIGHEST) + b_ref[...]


def _modulation(c_all, w_mod, b_mod):
    depth, d, n = w_mod.shape
    tn = 1536
    return pl.pallas_call(
        _mod_kernel,
        grid=(depth, n // tn),
        in_specs=[pl.BlockSpec((8, d), lambda l, j: (0, 0)),
                  pl.BlockSpec((None, d, tn), lambda l, j: (l, 0, j)),
                  pl.BlockSpec((None, 1, tn), lambda l, j: (l, 0, j))],
        out_specs=pl.BlockSpec((None, 8, tn), lambda l, j: (l, 0, j)),
        out_shape=jax.ShapeDtypeStruct((depth, 8, n), F32),
        compiler_params=_params(("arbitrary", "arbitrary")),
        name="modulation",
    )(c_all, w_mod, b_mod.reshape(depth, 1, n))


def _modulated_norm(x, g, shift, scale):
    ms = jnp.mean(x * x, axis=-1, keepdims=True)
    return x * lax.rsqrt(ms + EPS) * g * (1.0 + scale) + shift


def _head_norm(q, gsum, gain):
    sq = q * q
    hi = sq.astype(BF16)
    lo = (sq - hi.astype(F32)).astype(BF16)
    ss = (_dot(hi, gsum) + _dot(lo, gsum)) * (1.0 / HEAD_DIM)
    return q * lax.rsqrt(ss + EPS) * gain


def _rope(q, cos, sin_signed):
    n = q.shape[-1]
    lane = lax.broadcasted_iota(jnp.int32, q.shape, 1)
    swapped = jnp.where(lane % 2 == 0, pltpu.roll(q, n - 1, 1), pltpu.roll(q, 1, 1))
    return q * cos + swapped * sin_signed


def _inproj_kernel(x_ref, mod_ref, g_ref, w_ref, cq_ref, sq_ref, ck_ref, sk_ref, qg_ref, kg_ref,
                   gsum_ref, dcs_ref,
                   q_ref, k_ref, v_ref, nq_ref, nk_ref, nv_ref, fab_ref, cu_ref, gl_ref):
    mod = mod_ref[...]
    h = _modulated_norm(x_ref[...], g_ref[...], mod[0:1], mod[1:2]).astype(BF16)

    def seg(off, width):
        return _dot(h, w_ref[:, off:off + width])

    gsum = gsum_ref[...]
    q = _head_norm(seg(_OFF_GQ, 512), gsum, qg_ref[...])
    q_ref[...] = _rope(q, cq_ref[...], sq_ref[...]).astype(BF16)
    kv = seg(_OFF_GK, 256)
    k = _head_norm(kv[:, :128], gsum[:128, :128], kg_ref[...])
    k_ref[...] = _rope(k, ck_ref[...], sk_ref[...]).astype(BF16)
    v_ref[...] = kv[:, 128:].astype(BF16)
    nq_ref[...] = (seg(_OFF_NQ, 512) * Q_SCALE).astype(BF16)
    nk_ref[...] = seg(_OFF_NK, 512).astype(BF16)
    nv_ref[...] = seg(_OFF_NV, 512).astype(BF16)
    ab = _dot(seg(_OFF_FU, 512).astype(BF16), dcs_ref[...])
    fab_ref[0] = ab[:, :512].astype(BF16)
    fab_ref[1] = ab[:, 512:].astype(BF16)
    for j in range(2):
        cu_ref[:, j * 512:(j + 1) * 512] = seg(_OFF_CU + j * 512, 512).astype(BF16)
    for j in range(4):
        gl_ref[:, j * 1024:(j + 1) * 1024] = seg(_OFF_GL + j * 1024, 1024).astype(BF16)


def _inproj(xa, modtab, g1, w_in, tabs, consts, n_lat):
    b, nt, d = xa.shape
    tm = ROW_TILE
    cq, sq, ck, sk = tabs
    qg, kg, gsum, dcs = consts
    row = lambda width: pl.BlockSpec((None, tm, width), lambda bb, i: (bb, i, 0))
    tab = lambda width: pl.BlockSpec((tm, width), lambda bb, i: (i, 0))
    out_shapes = [jax.ShapeDtypeStruct((b, nt, wd), BF16) for wd in (512, 128, 128, 512, 512, 512)]
    out_shapes += [jax.ShapeDtypeStruct((b, 2, nt, 512), BF16),
                   jax.ShapeDtypeStruct((b, nt, 1024), BF16),
                   jax.ShapeDtypeStruct((b, nt, 4096), BF16)]
    out_specs = [row(512), row(128), row(128), row(512), row(512), row(512),
                 pl.BlockSpec((None, 2, tm, 512), lambda bb, i: (bb, 0, i, 0)),
                 row(1024), row(4096)]
    return pl.pallas_call(
        _inproj_kernel,
        grid=(b, nt // tm),
        in_specs=[row(d),
                  pl.BlockSpec((None, None, 6, d), lambda bb, i: (bb, i // n_lat, 0, 0)),
                  _const_spec((1, d)),
                  _const_spec((d, IN_WIDTH)),
                  tab(512), tab(512), tab(128), tab(128),
                  _const_spec((1, 512)), _const_spec((1, 128)),
                  _const_spec((512, 512)), _const_spec((512, 1024))],
        out_specs=out_specs,
        out_shape=out_shapes,
        compiler_params=_params(("parallel", "parallel")),
        name="inproj",
    )(xa, modtab, g1, w_in, cq, sq, ck, sk, qg, kg, gsum, dcs)


def _with_ones(v):
    return jnp.concatenate([v, jnp.ones_like(v)], axis=1)


def _softmax_pv(parts):
    m = None
    for s, _ in parts:
        mi = jnp.max(s, axis=-1, keepdims=True)
        m = mi if m is None else jnp.maximum(m, mi)
    acc = None
    for s, v in parts:
        oi = _dot(jnp.exp2(s - m).astype(BF16), v)
        acc = oi if acc is None else acc + oi
    return acc[:, :HEAD_DIM] / acc[:, HEAD_DIM:HEAD_DIM + 1]


def _gqa_kernel(q_ref, k_ref, v_ref, o_ref, *, seq, ctx, n_lat):
    i = pl.program_id(1)
    group = N_HEADS // GQA_KV_HEADS

    def attend(lo, n):
        for kv in range(GQA_KV_HEADS):
            c0 = kv * HEAD_DIM
            kk = k_ref[lo:lo + n, c0:c0 + HEAD_DIM]
            vv = _with_ones(v_ref[lo:lo + n, c0:c0 + HEAD_DIM])
            for h in range(kv * group, (kv + 1) * group):
                qh = q_ref[:, h * HEAD_DIM:(h + 1) * HEAD_DIM]
                o = _softmax_pv([(_dot_t(qh, kk), vv)])
                o_ref[:, h * HEAD_DIM:(h + 1) * HEAD_DIM] = o.astype(o_ref.dtype)

    @pl.when(i < n_lat)
    def _():
        attend(0, seq + ctx)

    @pl.when(i >= n_lat)
    def _():
        attend(seq, ctx)


def _gqa(q, k, v, seq, ctx, n_tiles):
    b, nt, _ = q.shape
    tm = ROW_TILE
    n_lat = seq // tm
    return pl.pallas_call(
        functools.partial(_gqa_kernel, seq=seq, ctx=ctx, n_lat=n_lat),
        grid=(b, n_tiles),
        in_specs=[pl.BlockSpec((None, tm, 512), lambda bb, i: (bb, i, 0)),
                  pl.BlockSpec((None, nt, 128), lambda bb, i: (bb, 0, 0)),
                  pl.BlockSpec((None, nt, 128), lambda bb, i: (bb, 0, 0))],
        out_specs=pl.BlockSpec((None, tm, 512), lambda bb, i: (bb, i, 0)),
        out_shape=jax.ShapeDtypeStruct((b, n_tiles * tm, 512), BF16),
        compiler_params=_params(("parallel", "parallel")),
        name="gqa_attention",
    )(q, k, v)


def _na_kernel(q_ref, k_ref, v_ref, bias_ref, o_ref, *, seq, ctx, n_lat):
    i = pl.program_id(1)
    band = NA_BAND_ROWS * GRID_W
    rows = seq // GRID_W

    @pl.when(i < n_lat)
    def _():
        band_row = jnp.clip(i * NA_BLOCK_ROWS - NA_WIN_ROWS // 2, 0, rows - NA_BAND_ROWS)
        start = pl.multiple_of(band_row * GRID_W, ROW_TILE)
        for h in range(N_HEADS):
            hs = slice(h * HEAD_DIM, (h + 1) * HEAD_DIM)
            qh = q_ref[:, hs]
            s_band = _dot_t(qh, k_ref[pl.ds(start, band), hs]) + bias_ref[h]
            s_ctx = _dot_t(qh, k_ref[seq:seq + ctx, hs])
            o = _softmax_pv([(s_band, _with_ones(v_ref[pl.ds(start, band), hs])),
                             (s_ctx, _with_ones(v_ref[seq:seq + ctx, hs]))])
            o_ref[:, hs] = o.astype(o_ref.dtype)

    @pl.when(i >= n_lat)
    def _():
        for h in range(N_HEADS):
            hs = slice(h * HEAD_DIM, (h + 1) * HEAD_DIM)
            o = _softmax_pv([(_dot_t(q_ref[:, hs], k_ref[seq:seq + ctx, hs]), _with_ones(v_ref[seq:seq + ctx, hs]))])
            o_ref[:, hs] = o.astype(o_ref.dtype)


def _na(q, k, v, bias, seq, ctx, n_tiles):
    b, nt, _ = q.shape
    tm = ROW_TILE
    n_lat = seq // tm
    band = NA_BAND_ROWS * GRID_W

    def bias_idx(bb, i):
        return (jnp.where(i == 0, 0, jnp.where(i >= n_lat - 1, 2, 1)), 0, 0, 0)

    return pl.pallas_call(
        functools.partial(_na_kernel, seq=seq, ctx=ctx, n_lat=n_lat),
        grid=(b, n_tiles),
        in_specs=[pl.BlockSpec((None, tm, 512), lambda bb, i: (bb, i, 0)),
                  pl.BlockSpec((None, nt, 512), lambda bb, i: (bb, 0, 0)),
                  pl.BlockSpec((None, nt, 512), lambda bb, i: (bb, 0, 0)),
                  pl.BlockSpec((None, N_HEADS, tm, band), bias_idx)],
        out_specs=pl.BlockSpec((None, tm, 512), lambda bb, i: (bb, i, 0)),
        out_shape=jax.ShapeDtypeStruct((b, n_tiles * tm, 512), BF16),
        compiler_params=_params(("parallel", "parallel")),
        name="neighbourhood_attention",
    )(q, k, v, bias)


def _bias_rows_kernel(rpb_ref, onehot_ref, mask_ref, o_ref):
    o_ref[...] = jnp.dot(rpb_ref[...], onehot_ref[...], preferred_element_type=F32,
                         precision=lax.Precision.HIGHEST) + mask_ref[...]


def _na_bias(rpb, seq):
    n_dr, n_dc = 2 * NA_WIN_ROWS - 1, 2 * NA_WIN_COLS - 1
    qc = np.arange(GRID_W)
    c0 = np.clip(qc - NA_WIN_COLS // 2, 0, GRID_W - NA_WIN_COLS)
    col_ok = (qc[None, :] >= c0[:, None]) & (qc[None, :] < c0[:, None] + NA_WIN_COLS)
    col_idx = qc[None, :] - qc[:, None] + NA_WIN_COLS - 1
    onehot = np.zeros((128, GRID_W * GRID_W), np.float32)
    for j in range(n_dc):
        onehot[j] = ((col_idx == j) & col_ok).reshape(-1)
    mask = np.where(col_ok, 0.0, MASK_VALUE).astype(np.float32).reshape(1, -1)
    rpb_rows = jnp.zeros((128, 128), F32).at[:N_HEADS * n_dr, :n_dc].set(rpb.astype(F32).reshape(-1, n_dc))
    blocks = pl.pallas_call(
        _bias_rows_kernel,
        out_shape=jax.ShapeDtypeStruct((128, GRID_W * GRID_W), F32),
        name="na_bias_rows",
    )(rpb_rows, jnp.asarray(onehot), jnp.asarray(mask))
    blocks = (blocks[:N_HEADS * n_dr] * LOG2E).reshape(N_HEADS, n_dr, GRID_W, GRID_W)
    masked = jnp.full((N_HEADS, GRID_W, GRID_W), MASK_VALUE, F32)

    rows = seq // GRID_W
    n_blk = rows // NA_BLOCK_ROWS
    tables = []
    for blk in (0, 1, n_blk - 1):
        band_row = int(np.clip(blk * NA_BLOCK_ROWS - NA_WIN_ROWS // 2, 0, rows - NA_BAND_ROWS))
        strips = []
        for qr in range(blk * NA_BLOCK_ROWS, (blk + 1) * NA_BLOCK_ROWS):
            r0 = int(np.clip(qr - NA_WIN_ROWS // 2, 0, rows - NA_WIN_ROWS))
            strip = [blocks[:, kr - qr + NA_WIN_ROWS - 1] if r0 <= kr < r0 + NA_WIN_ROWS else masked
                     for kr in range(band_row, band_row + NA_BAND_ROWS)]
            strips.append(jnp.concatenate(strip, axis=-1))
        tables.append(jnp.concatenate(strips, axis=1))
    return jnp.stack(tables)


def _dft_kernel(wc_ref, ws_ref, a_ref, b_ref, o_ref):
    y = _dot(wc_ref[...], a_ref[...]) - _dot(ws_ref[...], b_ref[...])
    o_ref[...] = y.astype(o_ref.dtype)


def _dft_mats(n):
    r = int(round(np.sqrt(n)))
    assert r * r == n
    k = jnp.arange(n, dtype=jnp.int32)[:, None]
    j = jnp.arange(r, dtype=jnp.int32)[None, :]
    a = ((k * j) % r).astype(F32) * (2.0 * np.pi / r)
    bb = ((k * j) % n).astype(F32) * (2.0 * np.pi / n)
    ca, sa = jnp.cos(a)[:, :, None], jnp.sin(a)[:, :, None]
    cb, sb = jnp.cos(bb)[:, None, :], jnp.sin(bb)[:, None, :]
    scale = 1.0 / np.sqrt(n * FOURIER_GROUP_DIM)
    cos = ((ca * cb - sa * sb) * scale).reshape(n, n)
    sin = ((sa * cb + ca * sb) * scale).reshape(n, n)
    return cos.astype(BF16), sin.astype(BF16)


def _fourier_latent(fab, seq, mats):
    b, _, nt, w = fab.shape
    tf = 512 if seq % 512 == 0 else ROW_TILE
    wc, ws = mats
    return pl.pallas_call(
        _dft_kernel,
        grid=(seq // tf, b),
        in_specs=[pl.BlockSpec((tf, seq), lambda i, bb: (i, 0)),
                  pl.BlockSpec((tf, seq), lambda i, bb: (i, 0)),
                  pl.BlockSpec((None, None, seq, w), lambda i, bb: (bb, 0, 0, 0)),
                  pl.BlockSpec((None, None, seq, w), lambda i, bb: (bb, 1, 0, 0))],
        out_specs=pl.BlockSpec((None, tf, w), lambda i, bb: (bb, i, 0)),
        out_shape=jax.ShapeDtypeStruct((b, seq, w), BF16),
        compiler_params=_params(("parallel", "parallel")),
        name="fourier_latent",
    )(wc, ws, fab, fab)


def _fourier_ctx(fab, seq, ctx, mats):
    b, _, nt, w = fab.shape
    blk = seq // ctx
    wc, ws = mats
    return pl.pallas_call(
        _dft_kernel,
        grid=(b,),
        in_specs=[_const_spec((ctx, ctx)), _const_spec((ctx, ctx)),
                  pl.BlockSpec((None, None, ctx, w), lambda bb: (bb, 0, blk, 0)),
                  pl.BlockSpec((None, None, ctx, w), lambda bb: (bb, 1, blk, 0))],
        out_specs=pl.BlockSpec((None, ctx, w), lambda bb: (bb, 0, 0)),
        out_shape=jax.ShapeDtypeStruct((b, ctx, w), BF16),
        compiler_params=_params(("parallel",)),
        name="fourier_ctx",
    )(wc, ws, fab, fab)


def _conv_kernel(cu_ref, w_ref, b_ref, g_ref, o_ref, z_ref, *, n_lat, n_tiles):
    tm = ROW_TILE
    halo = CONV_HALO
    zeros = jnp.zeros((halo, MIX_WIDTH), F32)
    lat_end = halo + n_lat * tm
    z_ref[0:halo, :] = zeros
    z_ref[lat_end:lat_end + halo, :] = zeros
    ctx_end = lat_end + halo + (n_tiles - n_lat) * tm
    z_ref[ctx_end:ctx_end + halo, :] = zeros

    def tile_base(t):
        return pl.multiple_of(t * tm + halo + jnp.where(t >= n_lat, halo, 0), 8)

    def glu(t, carry):
        r = pl.multiple_of(t * tm, tm)
        a = cu_ref[pl.ds(r, tm), 0:MIX_WIDTH].astype(F32)
        g = cu_ref[pl.ds(r, tm), MIX_WIDTH:2 * MIX_WIDTH].astype(F32)
        z_ref[pl.ds(tile_base(t), tm), :] = a * _sigmoid(g)
        return carry

    lax.fori_loop(0, n_tiles, glu, 0)

    def conv(t, carry):
        start = pl.multiple_of(tile_base(t) - halo, 8)
        cols = []
        for c in range(MIX_WIDTH // 128):
            cs = slice(c * 128, (c + 1) * 128)
            win = z_ref[pl.ds(start, tm + 2 * halo), cs]
            acc = jnp.zeros((tm, 128), F32)
            for j in range(CONV_KERNEL):
                off = halo - CONV_KERNEL // 2 + j
                acc = acc + win[off:off + tm, :] * w_ref[j:j + 1, cs]
            cols.append(acc)
        y = jnp.concatenate(cols, axis=1) + b_ref[...]
        ms = jnp.mean(y * y, axis=-1, keepdims=True)
        y = y * lax.rsqrt(ms + EPS) * g_ref[...]
        r = pl.multiple_of(t * tm, tm)
        o_ref[pl.ds(r, tm), :] = (y * _sigmoid(y)).astype(o_ref.dtype)
        return carry

    lax.fori_loop(0, n_tiles, conv, 0)


def _conv(cu, conv_w, conv_b, conv_g, seq, n_tiles):
    b, nt, _ = cu.shape
    n_lat = seq // ROW_TILE
    return pl.pallas_call(
        functools.partial(_conv_kernel, n_lat=n_lat, n_tiles=n_tiles),
        grid=(b,),
        in_specs=[pl.BlockSpec((None, nt, 2 * MIX_WIDTH), lambda bb: (bb, 0, 0)),
                  _const_spec((CONV_KERNEL, MIX_WIDTH)),
                  _const_spec((1, MIX_WIDTH)), _const_spec((1, MIX_WIDTH))],
        out_specs=pl.BlockSpec((None, n_tiles * ROW_TILE, MIX_WIDTH), lambda bb: (bb, 0, 0)),
        out_shape=jax.ShapeDtypeStruct((b, n_tiles * ROW_TILE, MIX_WIDTH), BF16),
        scratch_shapes=[pltpu.VMEM((nt + 3 * CONV_HALO, MIX_WIDTH), F32)],
        compiler_params=_params(("parallel",)),
        name="conformer_conv",
    )(cu, conv_w, conv_b, conv_g)


_PAIRS = tuple((a, b) for a in range(EXPERTS_PER_GROUP) for b in range(a + 1, EXPERTS_PER_GROUP))
N_CLASSES = (N_EXPERTS // EXPERTS_PER_GROUP) * len(_PAIRS)


def _route(logits_t, bias):
    score = _sigmoid_exp(logits_t)
    sel = score + bias
    sel_r = [sel[e:e + 1] for e in range(N_EXPERTS)]
    n_groups = N_EXPERTS // EXPERTS_PER_GROUP

    def beats(a, ia, b, ib):
        return (a >= b) if ia < ib else (a > b)

    picked = []
    for g in range(n_groups):
        ids = range(g * EXPERTS_PER_GROUP, (g + 1) * EXPERTS_PER_GROUP)
        for e in ids:
            rank = sum(beats(sel_r[o], o, sel_r[e], e).astype(F32) for o in ids if o != e)
            picked.append(rank < 2.0)
    group_score = []
    for g in range(n_groups):
        ids = range(g * EXPERTS_PER_GROUP, (g + 1) * EXPERTS_PER_GROUP)
        group_score.append(sum(jnp.where(picked[e], sel_r[e], 0.0) for e in ids))
    cls = s_lo = s_hi = 0.0
    for g in range(n_groups):
        rank = sum(beats(group_score[o], o, group_score[g], g).astype(F32) for o in range(n_groups) if o != g)
        best = rank < 1.0
        for pid, (a, b) in enumerate(_PAIRS):
            ea, eb = g * EXPERTS_PER_GROUP + a, g * EXPERTS_PER_GROUP + b
            both = picked[ea] & picked[eb] & best
            cls = cls + jnp.where(both, float(g * len(_PAIRS) + pid), 0.0)
            s_lo = s_lo + jnp.where(both, score[ea:ea + 1], 0.0)
            s_hi = s_hi + jnp.where(both, score[eb:eb + 1], 0.0)
    denom = s_lo + s_hi
    return cls, s_lo / denom, s_hi / denom


def _merge_kernel(yg_ref, yn_ref, yf_ref, yfc_ref, yc_ref, gl_ref, x_ref, mod_ref, g2_ref, wb_ref, wo_ref,
                  wr_ref, rb_ref, xo_ref, row_ref, cls_ref, *, n_lat):
    y_fourier = jnp.where(pl.program_id(1) < n_lat, yf_ref[...], yfc_ref[...])
    acc = None
    for n, y in enumerate((yg_ref[...], yn_ref[...], y_fourier, yc_ref[...])):
        z = _dot(y, wb_ref[n])
        gate = _sigmoid(gl_ref[:, n * D_MODEL:(n + 1) * D_MODEL].astype(F32))
        acc = gate * z if acc is None else acc + gate * z
    mod = mod_ref[...]
    x = x_ref[...] + mod[2:3] * _dot(acc.astype(BF16), wo_ref[...])
    xo_ref[...] = x
    h2 = _modulated_norm(x, g2_ref[...], mod[3:4], mod[4:5])
    logits_t = lax.dot_general(wr_ref[...], h2, (((1,), (1,)), ((), ())),
                               preferred_element_type=F32, precision=lax.Precision.HIGHEST)
    cls, w_lo, w_hi = _route(logits_t, rb_ref[...])
    tm = h2.shape[0]
    cls_ref[...] = jnp.concatenate([cls, jnp.zeros((7, tm), F32)], axis=0)
    w_t = jnp.concatenate([w_lo, w_hi, jnp.zeros((ROW_EXTRA - 2, tm), F32)], axis=0).T
    row_ref[...] = jnp.concatenate([h2, w_t], axis=1)


def _merge(ys, gl, xa, modtab, g2, w_branch, w_out, w_router_t, router_bias, n_lat, n_tiles):
    b, nt, d = xa.shape
    tm = ROW_TILE
    row = lambda width: pl.BlockSpec((None, tm, width), lambda bb, i: (bb, i, 0))
    return pl.pallas_call(
        functools.partial(_merge_kernel, n_lat=n_lat),
        grid=(b, n_tiles),
        in_specs=[row(512), row(512),
                  pl.BlockSpec((None, tm, 512), lambda bb, i: (bb, jnp.minimum(i, n_lat - 1), 0)),
                  pl.BlockSpec((None, tm, 512), lambda bb, i: (bb, 0, 0)),
                  row(512), row(4096), row(d),
                  pl.BlockSpec((None, None, 6, d), lambda bb, i: (bb, i // n_lat, 0, 0)),
                  _const_spec((1, d)),
                  _const_spec((N_BRANCHES, MIX_WIDTH, d)),
                  _const_spec((d, d)),
                  _const_spec((N_EXPERTS, d)),
                  _const_spec((N_EXPERTS, 1))],
        out_specs=[row(d), row(d + ROW_EXTRA),
                   pl.BlockSpec((None, None, 8, tm), lambda bb, i: (bb, i, 0, 0))],
        out_shape=[jax.ShapeDtypeStruct((b, n_tiles * tm, d), F32),
                   jax.ShapeDtypeStruct((b, n_tiles * tm, d + ROW_EXTRA), F32),
                   jax.ShapeDtypeStruct((b, n_tiles, 8, tm), F32)],
        compiler_params=_params(("parallel", "parallel")),
        name="merge_router",
    )(*ys, gl, xa, modtab, g2, w_branch, w_out, w_router_t, router_bias)


def _plan_kernel(cls_ref, tri_ref, pos_ref, meta_ref, *, n_chunks):
    width = cls_ref.shape[1]
    kk = lax.broadcasted_iota(jnp.int32, (32, width), 0).astype(F32)

    def onehot(c):
        return jnp.where(cls_ref[c:c + 1, :] == kk, 1.0, 0.0)

    cnt = jnp.zeros((32, 1), F32)
    for c in range(n_chunks):
        cnt = cnt + jnp.sum(onehot(c), axis=1, keepdims=True)
    size = jnp.floor((cnt + (EXPERT_TILE - 1)) * (1.0 / EXPERT_TILE)) * EXPERT_TILE
    starts = []
    acc = jnp.zeros((1, 1), F32)
    for k in range(32):
        starts.append(acc)
        acc = acc + size[k:k + 1]
    base = jnp.concatenate(starts, axis=0)
    run = base
    for c in range(n_chunks):
        oh = onehot(c)
        before = _dot(oh.astype(BF16), tri_ref[...])
        pos_ref[c:c + 1, :] = jnp.sum(oh * (run + before), axis=0, keepdims=True).astype(jnp.int32)
        run = run + jnp.sum(oh, axis=1, keepdims=True)

    end = base + size
    tile_row = lax.broadcasted_iota(jnp.int32, (1, 128), 1).astype(F32) * EXPERT_TILE
    tcls = jnp.sum(jnp.where(end[:N_CLASSES] <= tile_row, 1.0, 0.0), axis=0, keepdims=True)
    tcls = jnp.minimum(tcls, N_CLASSES - 1.0)
    n_pairs = float(len(_PAIRS))
    grp = sum(jnp.where(tcls >= n_pairs * g, 1.0, 0.0) for g in range(1, N_EXPERTS // EXPERTS_PER_GROUP))
    pid = tcls - n_pairs * grp
    lo = sum(jnp.where(pid == float(i), float(a), 0.0) for i, (a, _) in enumerate(_PAIRS))
    hi = sum(jnp.where(pid == float(i), float(b), 0.0) for i, (_, b) in enumerate(_PAIRS))
    used = jnp.broadcast_to(end[N_CLASSES - 1:N_CLASSES] * (1.0 / EXPERT_TILE), (1, 128))
    rows = [grp * EXPERTS_PER_GROUP + lo, grp * EXPERTS_PER_GROUP + hi, used, jnp.zeros((5, 128), F32)]
    meta_ref[...] = jnp.concatenate(rows, axis=0).astype(jnp.int32)


def _plan(cls):
    n_chunks, width = cls.shape
    tri = jnp.asarray(np.triu(np.ones((width, width), np.float32), k=1)).astype(BF16)
    return pl.pallas_call(
        functools.partial(_plan_kernel, n_chunks=n_chunks),
        out_shape=[jax.ShapeDtypeStruct((n_chunks, width), jnp.int32),
                   jax.ShapeDtypeStruct((8, 128), jnp.int32)],
        name="moe_plan",
    )(cls, tri)


def _row_copy(src_ref, src_row, dst_ref, dst_row, sem):
    return pltpu.make_async_copy(src_ref.at[pl.ds(src_row, 1)], dst_ref.at[pl.ds(dst_row, 1)], sem)


def _dispatch_kernel(pos_ref, src_ref, init_ref, dst_ref, sem):
    del init_ref
    n_sub, width = pos_ref.shape
    t0 = pl.program_id(0) * (n_sub * width)
    for q in range(n_sub):
        def issue(r, carry, q=q):
            _row_copy(src_ref, t0 + q * width + r, dst_ref, pos_ref[q, r], sem).start()
            return carry
        lax.fori_loop(0, width, issue, 0, unroll=8)

    def drain(r, carry):
        _row_copy(src_ref, 0, dst_ref, 0, sem).wait()
        return carry
    lax.fori_loop(0, n_sub * width, drain, 0)


def _dispatch(pos, rows, n_sorted):
    n_steps, n_sub, width = pos.shape
    t, w = rows.shape
    assert t == n_steps * n_sub * width
    return pl.pallas_call(
        _dispatch_kernel,
        grid=(n_steps,),
        in_specs=[pl.BlockSpec((None, n_sub, width), lambda i: (i, 0, 0), memory_space=pltpu.SMEM),
                  pl.BlockSpec(memory_space=pl.ANY),
                  pl.BlockSpec(memory_space=pl.ANY)],
        out_specs=pl.BlockSpec(memory_space=pl.ANY),
        out_shape=jax.ShapeDtypeStruct((n_sorted, w), F32),
        scratch_shapes=[pltpu.SemaphoreType.DMA(())],
        input_output_aliases={2: 0},
        compiler_params=_params(("arbitrary",)),
        name="moe_dispatch",
    )(pos, rows, jnp.zeros((n_sorted, w), F32))


def _expert_pair_kernel(lo_ref, hi_ref, used_ref, xs_ref, gu_lo_ref, dn_lo_ref, gu_hi_ref, dn_hi_ref, o_ref):
    del lo_ref, hi_ref
    i = pl.program_id(0)

    @pl.when(i < used_ref[0])
    def _():
        x = xs_ref[:, :D_MODEL].astype(BF16)

        def ffn(gu_ref, dn_ref):
            ab = _dot(x, gu_ref[...])
            a = ab[:, :EXPERT_FF]
            return _dot((a * _sigmoid(a) * ab[:, EXPERT_FF:]).astype(BF16), dn_ref[...])

        o_ref[...] = (xs_ref[:, D_MODEL:D_MODEL + 1] * ffn(gu_lo_ref, dn_lo_ref)
                      + xs_ref[:, D_MODEL + 1:D_MODEL + 2] * ffn(gu_hi_ref, dn_hi_ref))

    @pl.when(i >= used_ref[0])
    def _():
        o_ref[...] = jnp.zeros_like(o_ref)


def _expert_pairs(meta, xs, w_gu, w_down):
    n_sorted, w = xs.shape
    d = D_MODEL
    te = EXPERT_TILE
    gu = lambda ref: pl.BlockSpec((None, d, 2 * EXPERT_FF), lambda i, lo, hi, used: (ref(lo, hi)[i], 0, 0))
    dn = lambda ref: pl.BlockSpec((None, EXPERT_FF, d), lambda i, lo, hi, used: (ref(lo, hi)[i], 0, 0))
    first, second = (lambda lo, hi: lo), (lambda lo, hi: hi)
    grid_spec = pltpu.PrefetchScalarGridSpec(
        num_scalar_prefetch=3,
        grid=(n_sorted // te,),
        in_specs=[pl.BlockSpec((te, w), lambda i, lo, hi, used: (i, 0)),
                  gu(first), dn(first), gu(second), dn(second)],
        out_specs=pl.BlockSpec((te, d), lambda i, lo, hi, used: (i, 0)))
    return pl.pallas_call(
        _expert_pair_kernel,
        grid_spec=grid_spec,
        out_shape=jax.ShapeDtypeStruct((n_sorted, d), F32),
        compiler_params=_params(("arbitrary",)),
        name="moe_expert_pairs",
    )(meta[0], meta[1], meta[2, :1], xs, w_gu, w_down, w_gu, w_down)


def _combine_kernel(pos_ref, x_ref, mod_ref, *rest, final):
    if final:
        g_ref, ys_ref, o_ref, z_ref, sem = rest
    else:
        ys_ref, o_ref, z_ref, sem = rest
    tm = z_ref.shape[0]

    def issue(r, carry):
        _row_copy(ys_ref, pos_ref[0, r], z_ref, r, sem).start()
        return carry
    lax.fori_loop(0, tm, issue, 0, unroll=8)

    def drain(r, carry):
        _row_copy(ys_ref, 0, z_ref, 0, sem).wait()
        return carry
    lax.fori_loop(0, tm, drain, 0)

    x = x_ref[...] + mod_ref[5:6, :] * z_ref[...]
    if final:
        ms = jnp.mean(x * x, axis=-1, keepdims=True)
        x = x * lax.rsqrt(ms + EPS) * g_ref[...]
    o_ref[...] = x


def _combine(pos, xa, ys, modtab, n_lat, final_g=None):
    b, rows, d = xa.shape
    tm = ROW_TILE
    n_tiles = rows // tm
    row = pl.BlockSpec((None, tm, d), lambda bb, i: (bb, i, 0))
    in_specs = [pl.BlockSpec((None, 1, tm), lambda bb, i: (bb * n_tiles + i, 0, 0), memory_space=pltpu.SMEM),
                row,
                pl.BlockSpec((None, None, 6, d), lambda bb, i: (bb, i // n_lat, 0, 0))]
    args = [pos, xa, modtab]
    if final_g is not None:
        in_specs.append(_const_spec((1, d)))
        args.append(final_g)
    in_specs.append(pl.BlockSpec(memory_space=pl.ANY))
    args.append(ys)
    return pl.pallas_call(
        functools.partial(_combine_kernel, final=final_g is not None),
        grid=(b, n_tiles),
        in_specs=in_specs,
        out_specs=row,
        out_shape=jax.ShapeDtypeStruct((b, rows, d), F32),
        scratch_shapes=[pltpu.VMEM((tm, d), F32), pltpu.SemaphoreType.DMA(())],
        compiler_params=_params(("arbitrary", "arbitrary")),
        name="moe_combine_final_norm" if final_g is not None else "moe_combine",
    )(*args)


def _moe(rows, cls, xa, modtab, w_gu, w_down, n_lat, final_g=None):
    b, n, w = rows.shape
    t = b * n
    n_chunks = t // ROW_TILE
    n_sub = DISPATCH_CHUNKS if n_chunks % DISPATCH_CHUNKS == 0 else 1
    n_sorted = t + N_CLASSES * EXPERT_TILE
    assert n_sorted // EXPERT_TILE <= 128
    pos, meta = _plan(cls[:, :, 0, :].reshape(n_chunks, ROW_TILE))
    xs = _dispatch(pos.reshape(n_chunks // n_sub, n_sub, ROW_TILE), rows.reshape(t, w), n_sorted)
    ys = _expert_pairs(meta, xs, w_gu, w_down)
    return _combine(pos.reshape(n_chunks, 1, ROW_TILE), xa, ys, modtab, n_lat, final_g)


def _rope_tables(seq, ctx):
    t = jnp.arange(seq, dtype=jnp.int32)
    row = (t // GRID_W).astype(F32)
    col = (t % GRID_W).astype(F32)
    n_pairs = HEAD_DIM // 4
    inv_freq = ROPE_THETA ** (-jnp.arange(n_pairs, dtype=F32) / n_pairs)
    ang = jnp.concatenate([row[:, None] * inv_freq, col[:, None] * inv_freq], axis=-1)
    cos = jnp.repeat(jnp.cos(ang), 2, axis=-1)
    sin = jnp.repeat(jnp.sin(ang), 2, axis=-1) * jnp.tile(jnp.array([-1.0, 1.0], F32), HEAD_DIM // 2)
    cos = jnp.concatenate([cos, jnp.ones((ctx, HEAD_DIM), F32)], axis=0)
    sin = jnp.concatenate([sin, jnp.zeros((ctx, HEAD_DIM), F32)], axis=0)
    ck, sk = jnp.tile(cos, (1, GQA_KV_HEADS)), jnp.tile(sin, (1, GQA_KV_HEADS))
    cq, sq = jnp.tile(cos, (1, N_HEADS)) * Q_SCALE, jnp.tile(sin, (1, N_HEADS)) * Q_SCALE
    return cq, sq, ck, sk


def _group_sum_matrix():
    idx = np.arange(MIX_WIDTH) // HEAD_DIM
    return jnp.asarray(idx[:, None] == idx[None, :], dtype=BF16)


def _channel_dft_matrix():
    c = np.arange(FOURIER_GROUP_DIM)
    ang = 2.0 * np.pi * ((c[:, None] * c[None, :]) % FOURIER_GROUP_DIM) / FOURIER_GROUP_DIM
    n_groups = MIX_WIDTH // FOURIER_GROUP_DIM
    eye = np.eye(n_groups)
    m = np.concatenate([np.kron(eye, np.cos(ang)), np.kron(eye, np.sin(ang))], axis=1)
    return jnp.asarray(m, dtype=F32).astype(BF16)


def kernel(x, c, ctx, c_ctx, w_mod, b_mod, norm1_g, norm2_g, w_in, q_norm_g, k_norm_g, na_rpb, conv_w, conv_b,
           conv_norm_g, w_branch, w_out, w_router, router_bias, w_expert_gu, w_expert_down, final_norm_g):
    b, seq, d = x.shape
    n_ctx = ctx.shape[1]
    depth = w_mod.shape[0]
    assert d == D_MODEL and seq % ROW_TILE == 0 and n_ctx == ROW_TILE and seq % n_ctx == 0
    assert seq // GRID_W >= NA_BAND_ROWS and b <= 7
    n_lat = seq // ROW_TILE
    n_all = n_lat + 1

    c_all = jnp.zeros((8, d), F32).at[:b].set(c).at[b].set(c_ctx)
    m = _modulation(c_all, w_mod, b_mod)
    m_lat = m[:, :b].reshape(depth, b, 1, 6, d)
    m_ctx = jnp.broadcast_to(m[:, b].reshape(depth, 1, 1, 6, d), (depth, b, 1, 6, d))
    modtab = jnp.concatenate([m_lat, m_ctx], axis=2)

    tabs = _rope_tables(seq, n_ctx)
    gsum = _group_sum_matrix()
    dcs = _channel_dft_matrix()
    dft_lat = _dft_mats(seq)
    dft_ctx = _dft_mats(n_ctx)
    w_router_t = w_router.T
    rb = router_bias.reshape(N_EXPERTS, 1)

    xa = jnp.concatenate([x, ctx], axis=1)
    out = None
    for l in range(depth):
        last = l == depth - 1
        n_tiles = n_lat if last else n_all
        qg = jnp.tile(q_norm_g[l], N_HEADS).reshape(1, -1)
        kg = jnp.tile(k_norm_g[l], GQA_KV_HEADS).reshape(1, -1)
        q, k, v, nq, nk, nv, fab, cu, gl = _inproj(
            xa, modtab[l], norm1_g[l].reshape(1, d), w_in[l].astype(BF16), tabs, (qg, kg, gsum, dcs), n_lat)
        y_gqa = _gqa(q, k, v, seq, n_ctx, n_tiles)
        y_na = _na(nq, nk, nv, _na_bias(na_rpb[l], seq), seq, n_ctx, n_tiles)
        y_fn = _fourier_latent(fab, seq, dft_lat)
        y_fn_ctx = y_fn if last else _fourier_ctx(fab, seq, n_ctx, dft_ctx)
        y_cv = _conv(cu, conv_w[l], conv_b[l].reshape(1, -1), conv_norm_g[l].reshape(1, -1), seq, n_tiles)
        xa, rows, cls = _merge((y_gqa, y_na, y_fn, y_fn_ctx, y_cv), gl, xa, modtab[l], norm2_g[l].reshape(1, d),
                               w_branch[l].astype(BF16), w_out[l].astype(BF16), w_router_t, rb, n_lat, n_tiles)
        xa = _moe(rows, cls, xa, modtab[l], w_expert_gu[l].astype(BF16), w_expert_down[l].astype(BF16), n_lat,
                  final_norm_g.reshape(1, d) if last else None)
    return xa
```

```python
import functools

import numpy as np
import jax
import jax.numpy as jnp
from jax import lax
from jax.experimental import pallas as pl
from jax.experimental.pallas import tpu as pltpu

D_MODEL = 1024
GRID_W = 64
MIX_WIDTH = 512
HEAD_DIM = 64
N_HEADS = 8
GQA_KV_HEADS = 2
NA_WIN_ROWS = 8
NA_WIN_COLS = 16
FOURIER_GROUP_DIM = 128
CONV_KERNEL = 31
N_BRANCHES = 4
N_EXPERTS = 16
EXPERTS_PER_GROUP = 4
EXPERT_FF = 512
ROPE_THETA = 10000.0
EPS = 1e-6
ATTN_SCALE = HEAD_DIM ** -0.5
LOG2E = float(np.log2(np.e))
Q_SCALE = ATTN_SCALE * LOG2E

_OFF_GQ, _OFF_GK, _OFF_GV = 0, 512, 640
_OFF_NQ, _OFF_NK, _OFF_NV = 768, 1280, 1792
_OFF_FU, _OFF_CU, _OFF_GL = 2304, 2816, 3840
IN_WIDTH = 7936

ROW_TILE = 256
NA_BLOCK_ROWS = ROW_TILE // GRID_W
NA_BAND_ROWS = NA_BLOCK_ROWS + NA_WIN_ROWS
ROW_EXTRA = 128
EXPERT_TILE = 256
DISPATCH_CHUNKS = 4
CONV_HALO = 16
MASK_VALUE = -1e30
VMEM_LIMIT = 56 * 1024 * 1024

F32 = jnp.float32
BF16 = jnp.bfloat16


def _params(sem, vmem=VMEM_LIMIT):
    return pltpu.CompilerParams(dimension_semantics=sem, vmem_limit_bytes=vmem)


def _const_spec(shape):
    nd = len(shape)
    return pl.BlockSpec(shape, lambda *_: (0,) * nd, pipeline_mode=pl.Buffered(1))


def _sigmoid(v):
    return 0.5 * jnp.tanh(0.5 * v) + 0.5


def _sigmoid_exp(v):
    return 1.0 / (1.0 + jnp.exp(-v))


def _dot(a, b):
    return jnp.dot(a, b, preferred_element_type=F32)


def _dot_t(a, b):
    return lax.dot_general(a, b, (((1,), (1,)), ((), ())), preferred_element_type=F32)


def _mod_kernel(c_ref, w_ref, b_ref, o_ref):
    cc = c_ref[...]
    a = cc * _sigmoid(cc)
    o_ref[...] = jnp.dot(a, w_ref[...], preferred_element_type=F32,
                         precision=lax.Precision.HIGHEST) + b_ref[...]


def _modulation(c_all, w_mod, b_mod):
    depth, d, n = w_mod.shape
    tn = 1536
    return pl.pallas_call(
        _mod_kernel,
        grid=(depth, n // tn),
        in_specs=[pl.BlockSpec((8, d), lambda l, j: (0, 0)),
                  pl.BlockSpec((None, d, tn), lambda l, j: (l, 0, j)),
                  pl.BlockSpec((None, 1, tn), lambda l, j: (l, 0, j))],
        out_specs=pl.BlockSpec((None, 8, tn), lambda l, j: (l, 0, j)),
        out_shape=jax.ShapeDtypeStruct((depth, 8, n), F32),
        compiler_params=_params(("arbitrary", "arbitrary")),
        name="modulation",
    )(c_all, w_mod, b_mod.reshape(depth, 1, n))


def _modulated_norm(x, g, shift, scale):
    ms = jnp.mean(x * x, axis=-1, keepdims=True)
    return x * lax.rsqrt(ms + EPS) * g * (1.0 + scale) + shift


def _head_norm(q, gsum, gain):
    sq = q * q
    hi = sq.astype(BF16)
    lo = (sq - hi.astype(F32)).astype(BF16)
    ss = (_dot(hi, gsum) + _dot(lo, gsum)) * (1.0 / HEAD_DIM)
    return q * lax.rsqrt(ss + EPS) * gain


def _rope(q, cos, sin_signed):
    n = q.shape[-1]
    lane = lax.broadcasted_iota(jnp.int32, q.shape, 1)
    swapped = jnp.where(lane % 2 == 0, pltpu.roll(q, n - 1, 1), pltpu.roll(q, 1, 1))
    return q * cos + swapped * sin_signed


def _inproj_kernel(x_ref, mod_ref, g_ref, w_ref, cq_ref, sq_ref, ck_ref, sk_ref, qg_ref, kg_ref,
                   gsum_ref, dcs_ref,
                   q_ref, k_ref, v_ref, nq_ref, nk_ref, nv_ref, fab_ref, cu_ref, gl_ref):
    mod = mod_ref[...]
    h = _modulated_norm(x_ref[...], g_ref[...], mod[0:1], mod[1:2]).astype(BF16)

    def seg(off, width):
        return _dot(h, w_ref[:, off:off + width])

    gsum = gsum_ref[...]
    q = _head_norm(seg(_OFF_GQ, 512), gsum, qg_ref[...])
    q_ref[...] = _rope(q, cq_ref[...], sq_ref[...]).astype(BF16)
    kv = seg(_OFF_GK, 256)
    k = _head_norm(kv[:, :128], gsum[:128, :128], kg_ref[...])
    k_ref[...] = _rope(k, ck_ref[...], sk_ref[...]).astype(BF16)
    v_ref[...] = kv[:, 128:].astype(BF16)
    nq_ref[...] = (seg(_OFF_NQ, 512) * Q_SCALE).astype(BF16)
    nk_ref[...] = seg(_OFF_NK, 512).astype(BF16)
    nv_ref[...] = seg(_OFF_NV, 512).astype(BF16)
    ab = _dot(seg(_OFF_FU, 512).astype(BF16), dcs_ref[...])
    fab_ref[0] = ab[:, :512].astype(BF16)
    fab_ref[1] = ab[:, 512:].astype(BF16)
    for j in range(2):
        cu_ref[:, j * 512:(j + 1) * 512] = seg(_OFF_CU + j * 512, 512).astype(BF16)
    for j in range(4):
        gl_ref[:, j * 1024:(j + 1) * 1024] = seg(_OFF_GL + j * 1024, 1024).astype(BF16)


def _inproj(xa, modtab, g1, w_in, tabs, consts, n_lat):
    b, nt, d = xa.shape
    tm = ROW_TILE
    cq, sq, ck, sk = tabs
    qg, kg, gsum, dcs = consts
    row = lambda width: pl.BlockSpec((None, tm, width), lambda bb, i: (bb, i, 0))
    tab = lambda width: pl.BlockSpec((tm, width), lambda bb, i: (i, 0))
    out_shapes = [jax.ShapeDtypeStruct((b, nt, wd), BF16) for wd in (512, 128, 128, 512, 512, 512)]
    out_shapes += [jax.ShapeDtypeStruct((b, 2, nt, 512), BF16),
                   jax.ShapeDtypeStruct((b, nt, 1024), BF16),
                   jax.ShapeDtypeStruct((b, nt, 4096), BF16)]
    out_specs = [row(512), row(128), row(128), row(512), row(512), row(512),
                 pl.BlockSpec((None, 2, tm, 512), lambda bb, i: (bb, 0, i, 0)),
                 row(1024), row(4096)]
    return pl.pallas_call(
        _inproj_kernel,
        grid=(b, nt // tm),
        in_specs=[row(d),
                  pl.BlockSpec((None, None, 6, d), lambda bb, i: (bb, i // n_lat, 0, 0)),
                  _const_spec((1, d)),
                  _const_spec((d, IN_WIDTH)),
                  tab(512), tab(512), tab(128), tab(128),
                  _const_spec((1, 512)), _const_spec((1, 128)),
                  _const_spec((512, 512)), _const_spec((512, 1024))],
        out_specs=out_specs,
        out_shape=out_shapes,
        compiler_params=_params(("parallel", "parallel")),
        name="inproj",
    )(xa, modtab, g1, w_in, cq, sq, ck, sk, qg, kg, gsum, dcs)


def _with_ones(v):
    return jnp.concatenate([v, jnp.ones_like(v)], axis=1)


def _softmax_pv(parts):
    m = None
    for s, _ in parts:
        mi = jnp.max(s, axis=-1, keepdims=True)
        m = mi if m is None else jnp.maximum(m, mi)
    acc = None
    for s, v in parts:
        oi = _dot(jnp.exp2(s - m).astype(BF16), v)
        acc = oi if acc is None else acc + oi
    return acc[:, :HEAD_DIM] / acc[:, HEAD_DIM:HEAD_DIM + 1]


def _gqa_kernel(q_ref, k_ref, v_ref, o_ref, *, seq, ctx, n_lat):
    i = pl.program_id(1)
    group = N_HEADS // GQA_KV_HEADS

    def attend(lo, n):
        for kv in range(GQA_KV_HEADS):
            c0 = kv * HEAD_DIM
            kk = k_ref[lo:lo + n, c0:c0 + HEAD_DIM]
            vv = _with_ones(v_ref[lo:lo + n, c0:c0 + HEAD_DIM])
            for h in range(kv * group, (kv + 1) * group):
                qh = q_ref[:, h * HEAD_DIM:(h + 1) * HEAD_DIM]
                o = _softmax_pv([(_dot_t(qh, kk), vv)])
                o_ref[:, h * HEAD_DIM:(h + 1) * HEAD_DIM] = o.astype(o_ref.dtype)

    @pl.when(i < n_lat)
    def _():
        attend(0, seq + ctx)

    @pl.when(i >= n_lat)
    def _():
        attend(seq, ctx)


def _gqa(q, k, v, seq, ctx, n_tiles):
    b, nt, _ = q.shape
    tm = ROW_TILE
    n_lat = seq // tm
    return pl.pallas_call(
        functools.partial(_gqa_kernel, seq=seq, ctx=ctx, n_lat=n_lat),
        grid=(b, n_tiles),
        in_specs=[pl.BlockSpec((None, tm, 512), lambda bb, i: (bb, i, 0)),
                  pl.BlockSpec((None, nt, 128), lambda bb, i: (bb, 0, 0)),
                  pl.BlockSpec((None, nt, 128), lambda bb, i: (bb, 0, 0))],
        out_specs=pl.BlockSpec((None, tm, 512), lambda bb, i: (bb, i, 0)),
        out_shape=jax.ShapeDtypeStruct((b, n_tiles * tm, 512), BF16),
        compiler_params=_params(("parallel", "parallel")),
        name="gqa_attention",
    )(q, k, v)


def _na_kernel(q_ref, k_ref, v_ref, bias_ref, o_ref, *, seq, ctx, n_lat):
    i = pl.program_id(1)
    band = NA_BAND_ROWS * GRID_W
    rows = seq // GRID_W

    @pl.when(i < n_lat)
    def _():
        band_row = jnp.clip(i * NA_BLOCK_ROWS - NA_WIN_ROWS // 2, 0, rows - NA_BAND_ROWS)
        start = pl.multiple_of(band_row * GRID_W, ROW_TILE)
        for h in range(N_HEADS):
            hs = slice(h * HEAD_DIM, (h + 1) * HEAD_DIM)
            qh = q_ref[:, hs]
            s_band = _dot_t(qh, k_ref[pl.ds(start, band), hs]) + bias_ref[h]
            s_ctx = _dot_t(qh, k_ref[seq:seq + ctx, hs])
            o = _softmax_pv([(s_band, _with_ones(v_ref[pl.ds(start, band), hs])),
                             (s_ctx, _with_ones(v_ref[seq:seq + ctx, hs]))])
            o_ref[:, hs] = o.astype(o_ref.dtype)

    @pl.when(i >= n_lat)
    def _():
        for h in range(N_HEADS):
            hs = slice(h * HEAD_DIM, (h + 1) * HEAD_DIM)
            o = _softmax_pv([(_dot_t(q_ref[:, hs], k_ref[seq:seq + ctx, hs]), _with_ones(v_ref[seq:seq + ctx, hs]))])
            o_ref[:, hs] = o.astype(o_ref.dtype)


def _na(q, k, v, bias, seq, ctx, n_tiles):
    b, nt, _ = q.shape
    tm = ROW_TILE
    n_lat = seq // tm
    band = NA_BAND_ROWS * GRID_W

    def bias_idx(bb, i):
        return (jnp.where(i == 0, 0, jnp.where(i >= n_lat - 1, 2, 1)), 0, 0, 0)

    return pl.pallas_call(
        functools.partial(_na_kernel, seq=seq, ctx=ctx, n_lat=n_lat),
        grid=(b, n_tiles),
        in_specs=[pl.BlockSpec((None, tm, 512), lambda bb, i: (bb, i, 0)),
                  pl.BlockSpec((None, nt, 512), lambda bb, i: (bb, 0, 0)),
                  pl.BlockSpec((None, nt, 512), lambda bb, i: (bb, 0, 0)),
                  pl.BlockSpec((None, N_HEADS, tm, band), bias_idx)],
        out_specs=pl.BlockSpec((None, tm, 512), lambda bb, i: (bb, i, 0)),
        out_shape=jax.ShapeDtypeStruct((b, n_tiles * tm, 512), BF16),
        compiler_params=_params(("parallel", "parallel")),
        name="neighbourhood_attention",
    )(q, k, v, bias)


def _bias_rows_kernel(rpb_ref, onehot_ref, mask_ref, o_ref):
    o_ref[...] = jnp.dot(rpb_ref[...], onehot_ref[...], preferred_element_type=F32,
                         precision=lax.Precision.HIGHEST) + mask_ref[...]


def _na_bias(rpb, seq):
    n_dr, n_dc = 2 * NA_WIN_ROWS - 1, 2 * NA_WIN_COLS - 1
    qc = np.arange(GRID_W)
    c0 = np.clip(qc - NA_WIN_COLS // 2, 0, GRID_W - NA_WIN_COLS)
    col_ok = (qc[None, :] >= c0[:, None]) & (qc[None, :] < c0[:, None] + NA_WIN_COLS)
    col_idx = qc[None, :] - qc[:, None] + NA_WIN_COLS - 1
    onehot = np.zeros((128, GRID_W * GRID_W), np.float32)
    for j in range(n_dc):
        onehot[j] = ((col_idx == j) & col_ok).reshape(-1)
    mask = np.where(col_ok, 0.0, MASK_VALUE).astype(np.float32).reshape(1, -1)
    rpb_rows = jnp.zeros((128, 128), F32).at[:N_HEADS * n_dr, :n_dc].set(rpb.astype(F32).reshape(-1, n_dc))
    blocks = pl.pallas_call(
        _bias_rows_kernel,
        out_shape=jax.ShapeDtypeStruct((128, GRID_W * GRID_W), F32),
        name="na_bias_rows",
    )(rpb_rows, jnp.asarray(onehot), jnp.asarray(mask))
    blocks = (blocks[:N_HEADS * n_dr] * LOG2E).reshape(N_HEADS, n_dr, GRID_W, GRID_W)
    masked = jnp.full((N_HEADS, GRID_W, GRID_W), MASK_VALUE, F32)

    rows = seq // GRID_W
    n_blk = rows // NA_BLOCK_ROWS
    tables = []
    for blk in (0, 1, n_blk - 1):
        band_row = int(np.clip(blk * NA_BLOCK_ROWS - NA_WIN_ROWS // 2, 0, rows - NA_BAND_ROWS))
        strips = []
        for qr in range(blk * NA_BLOCK_ROWS, (blk + 1) * NA_BLOCK_ROWS):
            r0 = int(np.clip(qr - NA_WIN_ROWS // 2, 0, rows - NA_WIN_ROWS))
            strip = [blocks[:, kr - qr + NA_WIN_ROWS - 1] if r0 <= kr < r0 + NA_WIN_ROWS else masked
                     for kr in range(band_row, band_row + NA_BAND_ROWS)]
            strips.append(jnp.concatenate(strip, axis=-1))
        tables.append(jnp.concatenate(strips, axis=1))
    return jnp.stack(tables)


def _dft_kernel(wc_ref, ws_ref, a_ref, b_ref, o_ref):
    y = _dot(wc_ref[...], a_ref[...]) - _dot(ws_ref[...], b_ref[...])
    o_ref[...] = y.astype(o_ref.dtype)


def _dft_mats(n):
    r = int(round(np.sqrt(n)))
    assert r * r == n
    k = jnp.arange(n, dtype=jnp.int32)[:, None]
    j = jnp.arange(r, dtype=jnp.int32)[None, :]
    a = ((k * j) % r).astype(F32) * (2.0 * np.pi / r)
    bb = ((k * j) % n).astype(F32) * (2.0 * np.pi / n)
    ca, sa = jnp.cos(a)[:, :, None], jnp.sin(a)[:, :, None]
    cb, sb = jnp.cos(bb)[:, None, :], jnp.sin(bb)[:, None, :]
    scale = 1.0 / np.sqrt(n * FOURIER_GROUP_DIM)
    cos = ((ca * cb - sa * sb) * scale).reshape(n, n)
    sin = ((sa * cb + ca * sb) * scale).reshape(n, n)
    return cos.astype(BF16), sin.astype(BF16)


def _fourier_latent(fab, seq, mats):
    b, _, nt, w = fab.shape
    tf = 512 if seq % 512 == 0 else ROW_TILE
    wc, ws = mats
    return pl.pallas_call(
        _dft_kernel,
        grid=(seq // tf, b),
        in_specs=[pl.BlockSpec((tf, seq), lambda i, bb: (i, 0)),
                  pl.BlockSpec((tf, seq), lambda i, bb: (i, 0)),
                  pl.BlockSpec((None, None, seq, w), lambda i, bb: (bb, 0, 0, 0)),
                  pl.BlockSpec((None, None, seq, w), lambda i, bb: (bb, 1, 0, 0))],
        out_specs=pl.BlockSpec((None, tf, w), lambda i, bb: (bb, i, 0)),
        out_shape=jax.ShapeDtypeStruct((b, seq, w), BF16),
        compiler_params=_params(("parallel", "parallel")),
        name="fourier_latent",
    )(wc, ws, fab, fab)


def _fourier_ctx(fab, seq, ctx, mats):
    b, _, nt, w = fab.shape
    blk = seq // ctx
    wc, ws = mats
    return pl.pallas_call(
        _dft_kernel,
        grid=(b,),
        in_specs=[_const_spec((ctx, ctx)), _const_spec((ctx, ctx)),
                  pl.BlockSpec((None, None, ctx, w), lambda bb: (bb, 0, blk, 0)),
                  pl.BlockSpec((None, None, ctx, w), lambda bb: (bb, 1, blk, 0))],
        out_specs=pl.BlockSpec((None, ctx, w), lambda bb: (bb, 0, 0)),
        out_shape=jax.ShapeDtypeStruct((b, ctx, w), BF16),
        compiler_params=_params(("parallel",)),
        name="fourier_ctx",
    )(wc, ws, fab, fab)


def _conv_kernel(cu_ref, w_ref, b_ref, g_ref, o_ref, z_ref, *, n_lat, n_tiles):
    tm = ROW_TILE
    halo = CONV_HALO
    zeros = jnp.zeros((halo, MIX_WIDTH), F32)
    lat_end = halo + n_lat * tm
    z_ref[0:halo, :] = zeros
    z_ref[lat_end:lat_end + halo, :] = zeros
    ctx_end = lat_end + halo + (n_tiles - n_lat) * tm
    z_ref[ctx_end:ctx_end + halo, :] = zeros

    def tile_base(t):
        return pl.multiple_of(t * tm + halo + jnp.where(t >= n_lat, halo, 0), 8)

    def glu(t, carry):
        r = pl.multiple_of(t * tm, tm)
        a = cu_ref[pl.ds(r, tm), 0:MIX_WIDTH].astype(F32)
        g = cu_ref[pl.ds(r, tm), MIX_WIDTH:2 * MIX_WIDTH].astype(F32)
        z_ref[pl.ds(tile_base(t), tm), :] = a * _sigmoid(g)
        return carry

    lax.fori_loop(0, n_tiles, glu, 0)

    def conv(t, carry):
        start = pl.multiple_of(tile_base(t) - halo, 8)
        cols = []
        for c in range(MIX_WIDTH // 128):
            cs = slice(c * 128, (c + 1) * 128)
            win = z_ref[pl.ds(start, tm + 2 * halo), cs]
            acc = jnp.zeros((tm, 128), F32)
            n_win = tm + 2 * halo
            for s in range(8):
                shifted = win if s == 0 else pltpu.roll(win, n_win - s, 0)
                for j in range(CONV_KERNEL):
                    off = halo - CONV_KERNEL // 2 + j
                    if off % 8 == s:
                        acc = acc + shifted[off - s:off - s + tm, :] * w_ref[j:j + 1, cs]
            cols.append(acc)
        y = jnp.concatenate(cols, axis=1) + b_ref[...]
        ms = jnp.mean(y * y, axis=-1, keepdims=True)
        y = y * lax.rsqrt(ms + EPS) * g_ref[...]
        r = pl.multiple_of(t * tm, tm)
        o_ref[pl.ds(r, tm), :] = (y * _sigmoid(y)).astype(o_ref.dtype)
        return carry

    lax.fori_loop(0, n_tiles, conv, 0)


def _conv(cu, conv_w, conv_b, conv_g, seq, n_tiles):
    b, nt, _ = cu.shape
    n_lat = seq // ROW_TILE
    return pl.pallas_call(
        functools.partial(_conv_kernel, n_lat=n_lat, n_tiles=n_tiles),
        grid=(b,),
        in_specs=[pl.BlockSpec((None, nt, 2 * MIX_WIDTH), lambda bb: (bb, 0, 0)),
                  _const_spec((CONV_KERNEL, MIX_WIDTH)),
                  _const_spec((1, MIX_WIDTH)), _const_spec((1, MIX_WIDTH))],
        out_specs=pl.BlockSpec((None, n_tiles * ROW_TILE, MIX_WIDTH), lambda bb: (bb, 0, 0)),
        out_shape=jax.ShapeDtypeStruct((b, n_tiles * ROW_TILE, MIX_WIDTH), BF16),
        scratch_shapes=[pltpu.VMEM((nt + 3 * CONV_HALO, MIX_WIDTH), F32)],
        compiler_params=_params(("parallel",)),
        name="conformer_conv",
    )(cu, conv_w, conv_b, conv_g)


_PAIRS = tuple((a, b) for a in range(EXPERTS_PER_GROUP) for b in range(a + 1, EXPERTS_PER_GROUP))
N_CLASSES = (N_EXPERTS // EXPERTS_PER_GROUP) * len(_PAIRS)


def _route(logits_t, bias):
    score = _sigmoid_exp(logits_t)
    sel = score + bias
    sel_r = [sel[e:e + 1] for e in range(N_EXPERTS)]
    n_groups = N_EXPERTS // EXPERTS_PER_GROUP

    def beats(a, ia, b, ib):
        return (a >= b) if ia < ib else (a > b)

    picked = []
    for g in range(n_groups):
        ids = range(g * EXPERTS_PER_GROUP, (g + 1) * EXPERTS_PER_GROUP)
        for e in ids:
            rank = sum(beats(sel_r[o], o, sel_r[e], e).astype(F32) for o in ids if o != e)
            picked.append(rank < 2.0)
    group_score = []
    for g in range(n_groups):
        ids = range(g * EXPERTS_PER_GROUP, (g + 1) * EXPERTS_PER_GROUP)
        group_score.append(sum(jnp.where(picked[e], sel_r[e], 0.0) for e in ids))
    cls = s_lo = s_hi = 0.0
    for g in range(n_groups):
        rank = sum(beats(group_score[o], o, group_score[g], g).astype(F32) for o in range(n_groups) if o != g)
        best = rank < 1.0
        for pid, (a, b) in enumerate(_PAIRS):
            ea, eb = g * EXPERTS_PER_GROUP + a, g * EXPERTS_PER_GROUP + b
            both = picked[ea] & picked[eb] & best
            cls = cls + jnp.where(both, float(g * len(_PAIRS) + pid), 0.0)
            s_lo = s_lo + jnp.where(both, score[ea:ea + 1], 0.0)
            s_hi = s_hi + jnp.where(both, score[eb:eb + 1], 0.0)
    denom = s_lo + s_hi
    return cls, s_lo / denom, s_hi / denom


def _merge_kernel(yg_ref, yn_ref, yf_ref, yfc_ref, yc_ref, gl_ref, x_ref, mod_ref, g2_ref, wb_ref, wo_ref,
                  wr_ref, rb_ref, xo_ref, row_ref, cls_ref, *, n_lat):
    y_fourier = jnp.where(pl.program_id(1) < n_lat, yf_ref[...], yfc_ref[...])
    acc = None
    for n, y in enumerate((yg_ref[...], yn_ref[...], y_fourier, yc_ref[...])):
        z = _dot(y, wb_ref[n])
        gate = _sigmoid(gl_ref[:, n * D_MODEL:(n + 1) * D_MODEL].astype(F32))
        acc = gate * z if acc is None else acc + gate * z
    mod = mod_ref[...]
    x = x_ref[...] + mod[2:3] * _dot(acc.astype(BF16), wo_ref[...])
    xo_ref[...] = x
    h2 = _modulated_norm(x, g2_ref[...], mod[3:4], mod[4:5])
    logits_t = lax.dot_general(wr_ref[...], h2, (((1,), (1,)), ((), ())),
                               preferred_element_type=F32, precision=lax.Precision.HIGHEST)
    cls, w_lo, w_hi = _route(logits_t, rb_ref[...])
    tm = h2.shape[0]
    cls_ref[...] = jnp.concatenate([cls, jnp.zeros((7, tm), F32)], axis=0)
    w_t = jnp.concatenate([w_lo, w_hi, jnp.zeros((ROW_EXTRA - 2, tm), F32)], axis=0).T
    row_ref[...] = jnp.concatenate([h2, w_t], axis=1)


def _merge(ys, gl, xa, modtab, g2, w_branch, w_out, w_router_t, router_bias, n_lat, n_tiles):
    b, nt, d = xa.shape
    tm = ROW_TILE
    row = lambda width: pl.BlockSpec((None, tm, width), lambda bb, i: (bb, i, 0))
    return pl.pallas_call(
        functools.partial(_merge_kernel, n_lat=n_lat),
        grid=(b, n_tiles),
        in_specs=[row(512), row(512),
                  pl.BlockSpec((None, tm, 512), lambda bb, i: (bb, jnp.minimum(i, n_lat - 1), 0)),
                  pl.BlockSpec((None, tm, 512), lambda bb, i: (bb, 0, 0)),
                  row(512), row(4096), row(d),
                  pl.BlockSpec((None, None, 6, d), lambda bb, i: (bb, i // n_lat, 0, 0)),
                  _const_spec((1, d)),
                  _const_spec((N_BRANCHES, MIX_WIDTH, d)),
                  _const_spec((d, d)),
                  _const_spec((N_EXPERTS, d)),
                  _const_spec((N_EXPERTS, 1))],
        out_specs=[row(d), row(d + ROW_EXTRA),
                   pl.BlockSpec((None, None, 8, tm), lambda bb, i: (bb, i, 0, 0))],
        out_shape=[jax.ShapeDtypeStruct((b, n_tiles * tm, d), F32),
                   jax.ShapeDtypeStruct((b, n_tiles * tm, d + ROW_EXTRA), F32),
                   jax.ShapeDtypeStruct((b, n_tiles, 8, tm), F32)],
        compiler_params=_params(("parallel", "parallel")),
        name="merge_router",
    )(*ys, gl, xa, modtab, g2, w_branch, w_out, w_router_t, router_bias)


def _plan_kernel(cls_ref, tri_ref, pos_ref, meta_ref, *, n_chunks):
    width = cls_ref.shape[1]
    kk = lax.broadcasted_iota(jnp.int32, (32, width), 0).astype(F32)

    def onehot(c):
        return jnp.where(cls_ref[c:c + 1, :] == kk, 1.0, 0.0)

    cnt = jnp.zeros((32, 1), F32)
    for c in range(n_chunks):
        cnt = cnt + jnp.sum(onehot(c), axis=1, keepdims=True)
    size = jnp.floor((cnt + (EXPERT_TILE - 1)) * (1.0 / EXPERT_TILE)) * EXPERT_TILE
    starts = []
    acc = jnp.zeros((1, 1), F32)
    for k in range(32):
        starts.append(acc)
        acc = acc + size[k:k + 1]
    base = jnp.concatenate(starts, axis=0)
    run = base
    for c in range(n_chunks):
        oh = onehot(c)
        before = _dot(oh.astype(BF16), tri_ref[...])
        pos_ref[c:c + 1, :] = jnp.sum(oh * (run + before), axis=0, keepdims=True).astype(jnp.int32)
        run = run + jnp.sum(oh, axis=1, keepdims=True)

    end = base + size
    tile_row = lax.broadcasted_iota(jnp.int32, (1, 128), 1).astype(F32) * EXPERT_TILE
    tcls = jnp.sum(jnp.where(end[:N_CLASSES] <= tile_row, 1.0, 0.0), axis=0, keepdims=True)
    tcls = jnp.minimum(tcls, N_CLASSES - 1.0)
    n_pairs = float(len(_PAIRS))
    grp = sum(jnp.where(tcls >= n_pairs * g, 1.0, 0.0) for g in range(1, N_EXPERTS // EXPERTS_PER_GROUP))
    pid = tcls - n_pairs * grp
    lo = sum(jnp.where(pid == float(i), float(a), 0.0) for i, (a, _) in enumerate(_PAIRS))
    hi = sum(jnp.where(pid == float(i), float(b), 0.0) for i, (_, b) in enumerate(_PAIRS))
    used = jnp.broadcast_to(end[N_CLASSES - 1:N_CLASSES] * (1.0 / EXPERT_TILE), (1, 128))
    rows = [grp * EXPERTS_PER_GROUP + lo, grp * EXPERTS_PER_GROUP + hi, used, jnp.zeros((5, 128), F32)]
    meta_ref[...] = jnp.concatenate(rows, axis=0).astype(jnp.int32)


def _plan(cls):
    n_chunks, width = cls.shape
    tri = jnp.asarray(np.triu(np.ones((width, width), np.float32), k=1)).astype(BF16)
    return pl.pallas_call(
        functools.partial(_plan_kernel, n_chunks=n_chunks),
        out_shape=[jax.ShapeDtypeStruct((n_chunks, width), jnp.int32),
                   jax.ShapeDtypeStruct((8, 128), jnp.int32)],
        name="moe_plan",
    )(cls, tri)


def _row_copy(src_ref, src_row, dst_ref, dst_row, sem):
    return pltpu.make_async_copy(src_ref.at[pl.ds(src_row, 1)], dst_ref.at[pl.ds(dst_row, 1)], sem)


def _dispatch_kernel(pos_ref, src_ref, init_ref, dst_ref, sem):
    del init_ref
    n_sub, width = pos_ref.shape
    for q in range(n_sub):
        def issue(r, carry, q=q):
            _row_copy(src_ref, q * width + r, dst_ref, pos_ref[q, r], sem).start()
            return carry
        lax.fori_loop(0, width, issue, 0, unroll=8)

    def drain(r, carry):
        _row_copy(src_ref, 0, dst_ref, 0, sem).wait()
        return carry
    lax.fori_loop(0, n_sub * width, drain, 0, unroll=8)


def _dispatch(pos, rows, n_sorted):
    n_steps, n_sub, width = pos.shape
    t, w = rows.shape
    assert t == n_steps * n_sub * width
    return pl.pallas_call(
        _dispatch_kernel,
        grid=(n_steps,),
        in_specs=[pl.BlockSpec((None, n_sub, width), lambda i: (i, 0, 0), memory_space=pltpu.SMEM),
                  pl.BlockSpec((n_sub * width, w), lambda i: (i, 0)),
                  pl.BlockSpec(memory_space=pl.ANY)],
        out_specs=pl.BlockSpec(memory_space=pl.ANY),
        out_shape=jax.ShapeDtypeStruct((n_sorted, w), F32),
        scratch_shapes=[pltpu.SemaphoreType.DMA(())],
        input_output_aliases={2: 0},
        compiler_params=_params(("arbitrary",)),
        name="moe_dispatch",
    )(pos, rows, jnp.zeros((n_sorted, w), F32))


def _expert_pair_kernel(lo_ref, hi_ref, used_ref, xs_ref, gu_lo_ref, dn_lo_ref, gu_hi_ref, dn_hi_ref, o_ref):
    del lo_ref, hi_ref
    i = pl.program_id(0)

    @pl.when(i < used_ref[0])
    def _():
        x = xs_ref[:, :D_MODEL].astype(BF16)

        def ffn(gu_ref, dn_ref):
            ab = _dot(x, gu_ref[...])
            a = ab[:, :EXPERT_FF]
            return _dot((a * _sigmoid(a) * ab[:, EXPERT_FF:]).astype(BF16), dn_ref[...])

        o_ref[...] = (xs_ref[:, D_MODEL:D_MODEL + 1] * ffn(gu_lo_ref, dn_lo_ref)
                      + xs_ref[:, D_MODEL + 1:D_MODEL + 2] * ffn(gu_hi_ref, dn_hi_ref))

    @pl.when(i >= used_ref[0])
    def _():
        o_ref[...] = jnp.zeros_like(o_ref)


def _expert_pairs(meta, xs, w_gu, w_down):
    n_sorted, w = xs.shape
    d = D_MODEL
    te = EXPERT_TILE
    gu = lambda ref: pl.BlockSpec((None, d, 2 * EXPERT_FF), lambda i, lo, hi, used: (ref(lo, hi)[i], 0, 0))
    dn = lambda ref: pl.BlockSpec((None, EXPERT_FF, d), lambda i, lo, hi, used: (ref(lo, hi)[i], 0, 0))
    first, second = (lambda lo, hi: lo), (lambda lo, hi: hi)
    grid_spec = pltpu.PrefetchScalarGridSpec(
        num_scalar_prefetch=3,
        grid=(n_sorted // te,),
        in_specs=[pl.BlockSpec((te, w), lambda i, lo, hi, used: (i, 0)),
                  gu(first), dn(first), gu(second), dn(second)],
        out_specs=pl.BlockSpec((te, d), lambda i, lo, hi, used: (i, 0)))
    return pl.pallas_call(
        _expert_pair_kernel,
        grid_spec=grid_spec,
        out_shape=jax.ShapeDtypeStruct((n_sorted, d), F32),
        compiler_params=_params(("arbitrary",)),
        name="moe_expert_pairs",
    )(meta[0], meta[1], meta[2, :1], xs, w_gu, w_down, w_gu, w_down)


def _combine_kernel(pos_ref, x_ref, g2_ref, *rest, final):
    if final:
        g_ref, ys_ref, o_ref, z_ref, sem = rest
    else:
        ys_ref, o_ref, z_ref, sem = rest
    n_sub, width = pos_ref.shape
    for q in range(n_sub):
        def issue(r, carry, q=q):
            _row_copy(ys_ref, pos_ref[q, r], z_ref, q * width + r, sem).start()
            return carry
        lax.fori_loop(0, width, issue, 0, unroll=8)

    def drain(r, carry):
        _row_copy(ys_ref, 0, z_ref, 0, sem).wait()
        return carry
    lax.fori_loop(0, n_sub * width, drain, 0, unroll=8)

    for q in range(n_sub):
        rows = slice(q * width, (q + 1) * width)
        x = x_ref[rows, :] + g2_ref[q] * z_ref[rows, :]
        if final:
            ms = jnp.mean(x * x, axis=-1, keepdims=True)
            x = x * lax.rsqrt(ms + EPS) * g_ref[...]
        o_ref[rows, :] = x


def _combine(pos, x, ys, g2, final_g=None):
    n_steps, n_sub, width = pos.shape
    t, d = x.shape
    tm = n_sub * width
    row = pl.BlockSpec((tm, d), lambda i: (i, 0))
    in_specs = [pl.BlockSpec((None, n_sub, width), lambda i: (i, 0, 0), memory_space=pltpu.SMEM),
                row,
                pl.BlockSpec((n_sub, 1, d), lambda i: (i, 0, 0))]
    args = [pos, x, g2]
    if final_g is not None:
        in_specs.append(_const_spec((1, d)))
        args.append(final_g)
    in_specs.append(pl.BlockSpec(memory_space=pl.ANY))
    args.append(ys)
    return pl.pallas_call(
        functools.partial(_combine_kernel, final=final_g is not None),
        grid=(n_steps,),
        in_specs=in_specs,
        out_specs=row,
        out_shape=jax.ShapeDtypeStruct((t, d), F32),
        scratch_shapes=[pltpu.VMEM((tm, d), F32), pltpu.SemaphoreType.DMA(())],
        compiler_params=_params(("arbitrary",)),
        name="moe_combine_final_norm" if final_g is not None else "moe_combine",
    )(*args)


def _moe(rows, cls, xa, modtab, w_gu, w_down, n_lat, final_g=None):
    b, n, w = rows.shape
    d = xa.shape[-1]
    t = b * n
    n_tiles = n // ROW_TILE
    n_chunks = b * n_tiles
    n_sub = DISPATCH_CHUNKS if n_chunks % DISPATCH_CHUNKS == 0 else 1
    n_sorted = t + N_CLASSES * EXPERT_TILE
    assert n_sorted // EXPERT_TILE <= 128
    pos, meta = _plan(cls[:, :, 0, :].reshape(n_chunks, ROW_TILE))
    pos = pos.reshape(n_chunks // n_sub, n_sub, ROW_TILE)
    xs = _dispatch(pos, rows.reshape(t, w), n_sorted)
    ys = _expert_pairs(meta, xs, w_gu, w_down)
    g2 = modtab[:, :, 5, :]
    g2 = jnp.concatenate([jnp.broadcast_to(g2[:, :1], (b, n_lat, d)), g2[:, 1:]], axis=1)[:, :n_tiles]
    out = _combine(pos, xa.reshape(t, d), ys, g2.reshape(n_chunks, 1, d), final_g)
    return out.reshape(b, n, d)


def _rope_tables(seq, ctx):
    t = jnp.arange(seq, dtype=jnp.int32)
    row = (t // GRID_W).astype(F32)
    col = (t % GRID_W).astype(F32)
    n_pairs = HEAD_DIM // 4
    inv_freq = ROPE_THETA ** (-jnp.arange(n_pairs, dtype=F32) / n_pairs)
    ang = jnp.concatenate([row[:, None] * inv_freq, col[:, None] * inv_freq], axis=-1)
    cos = jnp.repeat(jnp.cos(ang), 2, axis=-1)
    sin = jnp.repeat(jnp.sin(ang), 2, axis=-1) * jnp.tile(jnp.array([-1.0, 1.0], F32), HEAD_DIM // 2)
    cos = jnp.concatenate([cos, jnp.ones((ctx, HEAD_DIM), F32)], axis=0)
    sin = jnp.concatenate([sin, jnp.zeros((ctx, HEAD_DIM), F32)], axis=0)
    ck, sk = jnp.tile(cos, (1, GQA_KV_HEADS)), jnp.tile(sin, (1, GQA_KV_HEADS))
    cq, sq = jnp.tile(cos, (1, N_HEADS)) * Q_SCALE, jnp.tile(sin, (1, N_HEADS)) * Q_SCALE
    return cq, sq, ck, sk


def _group_sum_matrix():
    idx = np.arange(MIX_WIDTH) // HEAD_DIM
    return jnp.asarray(idx[:, None] == idx[None, :], dtype=BF16)


def _channel_dft_matrix():
    c = np.arange(FOURIER_GROUP_DIM)
    ang = 2.0 * np.pi * ((c[:, None] * c[None, :]) % FOURIER_GROUP_DIM) / FOURIER_GROUP_DIM
    n_groups = MIX_WIDTH // FOURIER_GROUP_DIM
    eye = np.eye(n_groups)
    m = np.concatenate([np.kron(eye, np.cos(ang)), np.kron(eye, np.sin(ang))], axis=1)
    return jnp.asarray(m, dtype=F32).astype(BF16)


def kernel(x, c, ctx, c_ctx, w_mod, b_mod, norm1_g, norm2_g, w_in, q_norm_g, k_norm_g, na_rpb, conv_w, conv_b,
           conv_norm_g, w_branch, w_out, w_router, router_bias, w_expert_gu, w_expert_down, final_norm_g):
    b, seq, d = x.shape
    n_ctx = ctx.shape[1]
    depth = w_mod.shape[0]
    assert d == D_MODEL and seq % ROW_TILE == 0 and n_ctx == ROW_TILE and seq % n_ctx == 0
    assert seq // GRID_W >= NA_BAND_ROWS and b <= 7
    n_lat = seq // ROW_TILE
    n_all = n_lat + 1

    c_all = jnp.zeros((8, d), F32).at[:b].set(c).at[b].set(c_ctx)
    m = _modulation(c_all, w_mod, b_mod)
    m_lat = m[:, :b].reshape(depth, b, 1, 6, d)
    m_ctx = jnp.broadcast_to(m[:, b].reshape(depth, 1, 1, 6, d), (depth, b, 1, 6, d))
    modtab = jnp.concatenate([m_lat, m_ctx], axis=2)

    tabs = _rope_tables(seq, n_ctx)
    gsum = _group_sum_matrix()
    dcs = _channel_dft_matrix()
    dft_lat = _dft_mats(seq)
    dft_ctx = _dft_mats(n_ctx)
    w_router_t = w_router.T
    rb = router_bias.reshape(N_EXPERTS, 1)

    xa = jnp.concatenate([x, ctx], axis=1)
    out = None
    for l in range(depth):
        last = l == depth - 1
        n_tiles = n_lat if last else n_all
        qg = jnp.tile(q_norm_g[l], N_HEADS).reshape(1, -1)
        kg = jnp.tile(k_norm_g[l], GQA_KV_HEADS).reshape(1, -1)
        q, k, v, nq, nk, nv, fab, cu, gl = _inproj(
            xa, modtab[l], norm1_g[l].reshape(1, d), w_in[l].astype(BF16), tabs, (qg, kg, gsum, dcs), n_lat)
        y_gqa = _gqa(q, k, v, seq, n_ctx, n_tiles)
        y_na = _na(nq, nk, nv, _na_bias(na_rpb[l], seq), seq, n_ctx, n_tiles)
        y_fn = _fourier_latent(fab, seq, dft_lat)
        y_fn_ctx = y_fn if last else _fourier_ctx(fab, seq, n_ctx, dft_ctx)
        y_cv = _conv(cu, conv_w[l], conv_b[l].reshape(1, -1), conv_norm_g[l].reshape(1, -1), seq, n_tiles)
        xa, rows, cls = _merge((y_gqa, y_na, y_fn, y_fn_ctx, y_cv), gl, xa, modtab[l], norm2_g[l].reshape(1, d),
                               w_branch[l].astype(BF16), w_out[l].astype(BF16), w_router_t, rb, n_lat, n_tiles)
        xa = _moe(rows, cls, xa, modtab[l], w_expert_gu[l].astype(BF16), w_expert_down[l].astype(BF16), n_lat,
                  final_norm_g.reshape(1, d) if last else None)
    return xa
```

```python
import functools

import numpy as np
import jax
import jax.numpy as jnp
from jax import lax
from jax.experimental import pallas as pl
from jax.experimental.pallas import tpu as pltpu

D_MODEL = 1024
GRID_W = 64
MIX_WIDTH = 512
HEAD_DIM = 64
N_HEADS = 8
GQA_KV_HEADS = 2
NA_WIN_ROWS = 8
NA_WIN_COLS = 16
FOURIER_GROUP_DIM = 128
CONV_KERNEL = 31
N_BRANCHES = 4
N_EXPERTS = 16
EXPERTS_PER_GROUP = 4
EXPERT_FF = 512
ROPE_THETA = 10000.0
EPS = 1e-6
ATTN_SCALE = HEAD_DIM ** -0.5
LOG2E = float(np.log2(np.e))
Q_SCALE = ATTN_SCALE * LOG2E

_OFF_GQ, _OFF_GK, _OFF_GV = 0, 512, 640
_OFF_NQ, _OFF_NK, _OFF_NV = 768, 1280, 1792
_OFF_FU, _OFF_CU, _OFF_GL = 2304, 2816, 3840
IN_WIDTH = 7936

ROW_TILE = 256
NA_BLOCK_ROWS = ROW_TILE // GRID_W
NA_BAND_ROWS = NA_BLOCK_ROWS + NA_WIN_ROWS
ROW_EXTRA = 128
EXPERT_TILE = 256
DISPATCH_CHUNKS = 4
CONV_HALO = 16
MASK_VALUE = -1e30
VMEM_LIMIT = 56 * 1024 * 1024

F32 = jnp.float32
BF16 = jnp.bfloat16


def _params(sem, vmem=VMEM_LIMIT):
    return pltpu.CompilerParams(dimension_semantics=sem, vmem_limit_bytes=vmem)


def _const_spec(shape):
    nd = len(shape)
    return pl.BlockSpec(shape, lambda *_: (0,) * nd, pipeline_mode=pl.Buffered(1))


def _layer_spec(shape, layer):
    nd = len(shape)
    return pl.BlockSpec((None,) + tuple(shape), lambda *_: (layer,) + (0,) * nd, pipeline_mode=pl.Buffered(1))


def _stream_specs(tm, width, n_lat, ctx_block):
    return [pl.BlockSpec((None, tm, width), lambda bb, i: (bb, jnp.minimum(i, n_lat - 1), 0)),
            pl.BlockSpec((None, tm, width), lambda bb, i: (bb, ctx_block, 0))]


def _stream_tile(lat_ref, ctx_ref, n_lat):
    return jnp.where(pl.program_id(1) < n_lat, lat_ref[...], ctx_ref[...])


def _sigmoid(v):
    return 0.5 * jnp.tanh(0.5 * v) + 0.5


def _sigmoid_exp(v):
    return 1.0 / (1.0 + jnp.exp(-v))


def _dot(a, b):
    return jnp.dot(a, b, preferred_element_type=F32)


def _dot_t(a, b):
    return lax.dot_general(a, b, (((1,), (1,)), ((), ())), preferred_element_type=F32)


def _mod_kernel(c_ref, w_ref, b_ref, o_ref):
    cc = c_ref[...]
    a = cc * _sigmoid(cc)
    o_ref[...] = jnp.dot(a, w_ref[...], preferred_element_type=F32,
                         precision=lax.Precision.HIGHEST) + b_ref[...]


def _modulation(c_all, w_mod, b_mod):
    depth, d, n = w_mod.shape
    tn = 1536
    return pl.pallas_call(
        _mod_kernel,
        grid=(depth, n // tn),
        in_specs=[pl.BlockSpec((8, d), lambda l, j: (0, 0)),
                  pl.BlockSpec((None, d, tn), lambda l, j: (l, 0, j)),
                  pl.BlockSpec((None, 1, tn), lambda l, j: (l, 0, j))],
        out_specs=pl.BlockSpec((None, 8, tn), lambda l, j: (l, 0, j)),
        out_shape=jax.ShapeDtypeStruct((depth, 8, n), F32),
        compiler_params=_params(("arbitrary", "arbitrary")),
        name="modulation",
    )(c_all, w_mod, b_mod.reshape(depth, 1, n))


def _modulated_norm(x, g, shift, scale):
    ms = jnp.mean(x * x, axis=-1, keepdims=True)
    return x * lax.rsqrt(ms + EPS) * g * (1.0 + scale) + shift


def _head_norm(q, gsum, gain):
    sq = q * q
    hi = sq.astype(BF16)
    lo = (sq - hi.astype(F32)).astype(BF16)
    ss = (_dot(hi, gsum) + _dot(lo, gsum)) * (1.0 / HEAD_DIM)
    return q * lax.rsqrt(ss + EPS) * gain


def _rope(q, cos, sin_signed):
    n = q.shape[-1]
    lane = lax.broadcasted_iota(jnp.int32, q.shape, 1)
    swapped = jnp.where(lane % 2 == 0, pltpu.roll(q, n - 1, 1), pltpu.roll(q, 1, 1))
    return q * cos + swapped * sin_signed


def _inproj_kernel(xl_ref, xc_ref, mod_ref, g_ref, w_ref, cq_ref, sq_ref, ck_ref, sk_ref, qg_ref, kg_ref,
                   gsum_ref, dcs_ref,
                   q_ref, k_ref, v_ref, nq_ref, nk_ref, nv_ref, fab_ref, cu_ref, gl_ref, *, n_lat):
    mod = mod_ref[...]
    h = _modulated_norm(_stream_tile(xl_ref, xc_ref, n_lat), g_ref[...], mod[0:1], mod[1:2]).astype(BF16)

    def seg(off, width):
        return _dot(h, w_ref[:, off:off + width])

    gsum = gsum_ref[...]
    q = _head_norm(seg(_OFF_GQ, 512), gsum, qg_ref[...])
    q_ref[...] = _rope(q, cq_ref[...], sq_ref[...]).astype(BF16)
    kv = seg(_OFF_GK, 256)
    k = _head_norm(kv[:, :128], gsum[:128, :128], kg_ref[...])
    k_ref[...] = _rope(k, ck_ref[...], sk_ref[...]).astype(BF16)
    v_ref[...] = kv[:, 128:].astype(BF16)
    nq_ref[...] = (seg(_OFF_NQ, 512) * Q_SCALE).astype(BF16)
    nk_ref[...] = seg(_OFF_NK, 512).astype(BF16)
    nv_ref[...] = seg(_OFF_NV, 512).astype(BF16)
    ab = _dot(seg(_OFF_FU, 512).astype(BF16), dcs_ref[...])
    fab_ref[0] = ab[:, :512].astype(BF16)
    fab_ref[1] = ab[:, 512:].astype(BF16)
    for j in range(2):
        cu_ref[:, j * 512:(j + 1) * 512] = seg(_OFF_CU + j * 512, 512).astype(BF16)
    for j in range(4):
        gl_ref[:, j * 1024:(j + 1) * 1024] = seg(_OFF_GL + j * 1024, 1024).astype(BF16)


def _inproj(stream, modtab, g1, w_in, layer, tabs, consts, n_lat):
    x_lat, x_ctx, ctx_block = stream
    b, _, d = x_lat.shape
    tm = ROW_TILE
    nt = (n_lat + 1) * tm
    cq, sq, ck, sk = tabs
    qg, kg, gsum, dcs = consts
    row = lambda width: pl.BlockSpec((None, tm, width), lambda bb, i: (bb, i, 0))
    tab = lambda width: pl.BlockSpec((tm, width), lambda bb, i: (i, 0))
    out_shapes = [jax.ShapeDtypeStruct((b, nt, wd), BF16) for wd in (512, 128, 128, 512, 512, 512)]
    out_shapes += [jax.ShapeDtypeStruct((b, 2, nt, 512), BF16),
                   jax.ShapeDtypeStruct((b, nt, 1024), BF16),
                   jax.ShapeDtypeStruct((b, nt, 4096), BF16)]
    out_specs = [row(512), row(128), row(128), row(512), row(512), row(512),
                 pl.BlockSpec((None, 2, tm, 512), lambda bb, i: (bb, 0, i, 0)),
                 row(1024), row(4096)]
    return pl.pallas_call(
        functools.partial(_inproj_kernel, n_lat=n_lat),
        grid=(b, nt // tm),
        in_specs=_stream_specs(tm, d, n_lat, ctx_block) + [
                  pl.BlockSpec((None, None, 6, d), lambda bb, i: (bb, i // n_lat, 0, 0)),
                  _const_spec((1, d)),
                  _layer_spec((d, IN_WIDTH), layer),
                  tab(512), tab(512), tab(128), tab(128),
                  _const_spec((1, 512)), _const_spec((1, 128)),
                  _const_spec((512, 512)), _const_spec((512, 1024))],
        out_specs=out_specs,
        out_shape=out_shapes,
        compiler_params=_params(("parallel", "parallel")),
        name="inproj",
    )(x_lat, x_ctx, modtab, g1, w_in, cq, sq, ck, sk, qg, kg, gsum, dcs)


def _with_ones(v):
    return jnp.concatenate([v, jnp.ones_like(v)], axis=1)


def _softmax_pv(parts):
    m = None
    for s, _ in parts:
        mi = jnp.max(s, axis=-1, keepdims=True)
        m = mi if m is None else jnp.maximum(m, mi)
    acc = None
    for s, v in parts:
        oi = _dot(jnp.exp2(s - m).astype(BF16), v)
        acc = oi if acc is None else acc + oi
    return acc[:, :HEAD_DIM] / acc[:, HEAD_DIM:HEAD_DIM + 1]


def _gqa_kernel(q_ref, k_ref, v_ref, o_ref, *, seq, ctx, n_lat):
    i = pl.program_id(1)
    group = N_HEADS // GQA_KV_HEADS

    def attend(lo, n):
        for kv in range(GQA_KV_HEADS):
            c0 = kv * HEAD_DIM
            kk = k_ref[lo:lo + n, c0:c0 + HEAD_DIM]
            vv = _with_ones(v_ref[lo:lo + n, c0:c0 + HEAD_DIM])
            for h in range(kv * group, (kv + 1) * group):
                qh = q_ref[:, h * HEAD_DIM:(h + 1) * HEAD_DIM]
                o = _softmax_pv([(_dot_t(qh, kk), vv)])
                o_ref[:, h * HEAD_DIM:(h + 1) * HEAD_DIM] = o.astype(o_ref.dtype)

    @pl.when(i < n_lat)
    def _():
        attend(0, seq + ctx)

    @pl.when(i >= n_lat)
    def _():
        attend(seq, ctx)


def _gqa(q, k, v, seq, ctx, n_tiles):
    b, nt, _ = q.shape
    tm = ROW_TILE
    n_lat = seq // tm
    return pl.pallas_call(
        functools.partial(_gqa_kernel, seq=seq, ctx=ctx, n_lat=n_lat),
        grid=(b, n_tiles),
        in_specs=[pl.BlockSpec((None, tm, 512), lambda bb, i: (bb, i, 0)),
                  pl.BlockSpec((None, nt, 128), lambda bb, i: (bb, 0, 0)),
                  pl.BlockSpec((None, nt, 128), lambda bb, i: (bb, 0, 0))],
        out_specs=pl.BlockSpec((None, tm, 512), lambda bb, i: (bb, i, 0)),
        out_shape=jax.ShapeDtypeStruct((b, n_tiles * tm, 512), BF16),
        compiler_params=_params(("parallel", "parallel")),
        name="gqa_attention",
    )(q, k, v)


def _na_kernel(q_ref, k_ref, v_ref, bias_ref, o_ref, *, seq, ctx, n_lat):
    i = pl.program_id(1)
    band = NA_BAND_ROWS * GRID_W
    rows = seq // GRID_W

    @pl.when(i < n_lat)
    def _():
        band_row = jnp.clip(i * NA_BLOCK_ROWS - NA_WIN_ROWS // 2, 0, rows - NA_BAND_ROWS)
        start = pl.multiple_of(band_row * GRID_W, ROW_TILE)
        for h in range(N_HEADS):
            hs = slice(h * HEAD_DIM, (h + 1) * HEAD_DIM)
            qh = q_ref[:, hs]
            s_band = _dot_t(qh, k_ref[pl.ds(start, band), hs]) + bias_ref[h]
            s_ctx = _dot_t(qh, k_ref[seq:seq + ctx, hs])
            o = _softmax_pv([(s_band, _with_ones(v_ref[pl.ds(start, band), hs])),
                             (s_ctx, _with_ones(v_ref[seq:seq + ctx, hs]))])
            o_ref[:, hs] = o.astype(o_ref.dtype)

    @pl.when(i >= n_lat)
    def _():
        for h in range(N_HEADS):
            hs = slice(h * HEAD_DIM, (h + 1) * HEAD_DIM)
            o = _softmax_pv([(_dot_t(q_ref[:, hs], k_ref[seq:seq + ctx, hs]), _with_ones(v_ref[seq:seq + ctx, hs]))])
            o_ref[:, hs] = o.astype(o_ref.dtype)


def _na(q, k, v, bias, layer, seq, ctx, n_tiles):
    b, nt, _ = q.shape
    tm = ROW_TILE
    n_lat = seq // tm
    band = NA_BAND_ROWS * GRID_W

    def bias_idx(bb, i):
        return (jnp.where(i == 0, 0, jnp.where(i >= n_lat - 1, 2, 1)), layer, 0, 0, 0)

    return pl.pallas_call(
        functools.partial(_na_kernel, seq=seq, ctx=ctx, n_lat=n_lat),
        grid=(b, n_tiles),
        in_specs=[pl.BlockSpec((None, tm, 512), lambda bb, i: (bb, i, 0)),
                  pl.BlockSpec((None, nt, 512), lambda bb, i: (bb, 0, 0)),
                  pl.BlockSpec((None, nt, 512), lambda bb, i: (bb, 0, 0)),
                  pl.BlockSpec((None, None, N_HEADS, tm, band), bias_idx)],
        out_specs=pl.BlockSpec((None, tm, 512), lambda bb, i: (bb, i, 0)),
        out_shape=jax.ShapeDtypeStruct((b, n_tiles * tm, 512), BF16),
        compiler_params=_params(("parallel", "parallel")),
        name="neighbourhood_attention",
    )(q, k, v, bias)


def _bias_rows_kernel(rpb_ref, onehot_ref, mask_ref, o_ref):
    o_ref[...] = jnp.dot(rpb_ref[...], onehot_ref[...], preferred_element_type=F32,
                         precision=lax.Precision.HIGHEST) + mask_ref[...]


def _na_bias(rpb, seq):
    n_dr, n_dc = 2 * NA_WIN_ROWS - 1, 2 * NA_WIN_COLS - 1
    qc = np.arange(GRID_W)
    c0 = np.clip(qc - NA_WIN_COLS // 2, 0, GRID_W - NA_WIN_COLS)
    col_ok = (qc[None, :] >= c0[:, None]) & (qc[None, :] < c0[:, None] + NA_WIN_COLS)
    col_idx = qc[None, :] - qc[:, None] + NA_WIN_COLS - 1
    onehot = np.zeros((128, GRID_W * GRID_W), np.float32)
    for j in range(n_dc):
        onehot[j] = ((col_idx == j) & col_ok).reshape(-1)
    mask = np.where(col_ok, 0.0, MASK_VALUE).astype(np.float32).reshape(1, -1)
    n_heads = rpb.shape[0] * rpb.shape[1]
    n_rows = -(-n_heads * n_dr // 128) * 128
    rpb_rows = jnp.zeros((n_rows, 128), F32).at[:n_heads * n_dr, :n_dc].set(rpb.astype(F32).reshape(-1, n_dc))
    blocks = pl.pallas_call(
        _bias_rows_kernel,
        out_shape=jax.ShapeDtypeStruct((n_rows, GRID_W * GRID_W), F32),
        name="na_bias_rows",
    )(rpb_rows, jnp.asarray(onehot), jnp.asarray(mask))
    blocks = (blocks[:n_heads * n_dr] * LOG2E).reshape(n_heads, n_dr, GRID_W, GRID_W)
    masked = jnp.full((n_heads, GRID_W, GRID_W), MASK_VALUE, F32)

    rows = seq // GRID_W
    n_blk = rows // NA_BLOCK_ROWS
    tables = []
    for blk in (0, 1, n_blk - 1):
        band_row = int(np.clip(blk * NA_BLOCK_ROWS - NA_WIN_ROWS // 2, 0, rows - NA_BAND_ROWS))
        strips = []
        for qr in range(blk * NA_BLOCK_ROWS, (blk + 1) * NA_BLOCK_ROWS):
            r0 = int(np.clip(qr - NA_WIN_ROWS // 2, 0, rows - NA_WIN_ROWS))
            strip = [blocks[:, kr - qr + NA_WIN_ROWS - 1] if r0 <= kr < r0 + NA_WIN_ROWS else masked
                     for kr in range(band_row, band_row + NA_BAND_ROWS)]
            strips.append(jnp.concatenate(strip, axis=-1))
        tables.append(jnp.concatenate(strips, axis=1))
    return jnp.stack(tables).reshape(3, rpb.shape[0], rpb.shape[1], ROW_TILE, NA_BAND_ROWS * GRID_W)


def _dft_kernel(wc_ref, ws_ref, a_ref, b_ref, o_ref):
    y = _dot(wc_ref[...], a_ref[...]) - _dot(ws_ref[...], b_ref[...])
    o_ref[...] = y.astype(o_ref.dtype)


def _dft_mats(n):
    r = int(round(np.sqrt(n)))
    assert r * r == n
    k = jnp.arange(n, dtype=jnp.int32)[:, None]
    j = jnp.arange(r, dtype=jnp.int32)[None, :]
    a = ((k * j) % r).astype(F32) * (2.0 * np.pi / r)
    bb = ((k * j) % n).astype(F32) * (2.0 * np.pi / n)
    ca, sa = jnp.cos(a)[:, :, None], jnp.sin(a)[:, :, None]
    cb, sb = jnp.cos(bb)[:, None, :], jnp.sin(bb)[:, None, :]
    scale = 1.0 / np.sqrt(n * FOURIER_GROUP_DIM)
    cos = ((ca * cb - sa * sb) * scale).reshape(n, n)
    sin = ((sa * cb + ca * sb) * scale).reshape(n, n)
    return cos.astype(BF16), sin.astype(BF16)


def _fourier_latent(fab, seq, mats):
    b, _, nt, w = fab.shape
    tf = 512 if seq % 512 == 0 else ROW_TILE
    wc, ws = mats
    return pl.pallas_call(
        _dft_kernel,
        grid=(seq // tf, b),
        in_specs=[pl.BlockSpec((tf, seq), lambda i, bb: (i, 0)),
                  pl.BlockSpec((tf, seq), lambda i, bb: (i, 0)),
                  pl.BlockSpec((None, None, seq, w), lambda i, bb: (bb, 0, 0, 0)),
                  pl.BlockSpec((None, None, seq, w), lambda i, bb: (bb, 1, 0, 0))],
        out_specs=pl.BlockSpec((None, tf, w), lambda i, bb: (bb, i, 0)),
        out_shape=jax.ShapeDtypeStruct((b, seq, w), BF16),
        compiler_params=_params(("parallel", "parallel")),
        name="fourier_latent",
    )(wc, ws, fab, fab)


def _fourier_ctx(fab, seq, ctx, mats):
    b, _, nt, w = fab.shape
    blk = seq // ctx
    wc, ws = mats
    return pl.pallas_call(
        _dft_kernel,
        grid=(b,),
        in_specs=[_const_spec((ctx, ctx)), _const_spec((ctx, ctx)),
                  pl.BlockSpec((None, None, ctx, w), lambda bb: (bb, 0, blk, 0)),
                  pl.BlockSpec((None, None, ctx, w), lambda bb: (bb, 1, blk, 0))],
        out_specs=pl.BlockSpec((None, ctx, w), lambda bb: (bb, 0, 0)),
        out_shape=jax.ShapeDtypeStruct((b, ctx, w), BF16),
        compiler_params=_params(("parallel",)),
        name="fourier_ctx",
    )(wc, ws, fab, fab)


def _conv_kernel(cu_ref, w_ref, b_ref, g_ref, o_ref, z_ref, *, n_lat, n_tiles):
    tm = ROW_TILE
    halo = CONV_HALO
    zeros = jnp.zeros((halo, MIX_WIDTH), F32)
    lat_end = halo + n_lat * tm
    z_ref[0:halo, :] = zeros
    z_ref[lat_end:lat_end + halo, :] = zeros
    ctx_end = lat_end + halo + (n_tiles - n_lat) * tm
    z_ref[ctx_end:ctx_end + halo, :] = zeros

    def tile_base(t):
        return pl.multiple_of(t * tm + halo + jnp.where(t >= n_lat, halo, 0), 8)

    def glu(t, carry):
        r = pl.multiple_of(t * tm, tm)
        a = cu_ref[pl.ds(r, tm), 0:MIX_WIDTH].astype(F32)
        g = cu_ref[pl.ds(r, tm), MIX_WIDTH:2 * MIX_WIDTH].astype(F32)
        z_ref[pl.ds(tile_base(t), tm), :] = a * _sigmoid(g)
        return carry

    lax.fori_loop(0, n_tiles, glu, 0)

    def conv(t, carry):
        start = pl.multiple_of(tile_base(t) - halo, 8)
        cols = []
        for c in range(MIX_WIDTH // 128):
            cs = slice(c * 128, (c + 1) * 128)
            win = z_ref[pl.ds(start, tm + 2 * halo), cs]
            acc = jnp.zeros((tm, 128), F32)
            n_win = tm + 2 * halo
            for s in range(8):
                shifted = win if s == 0 else pltpu.roll(win, n_win - s, 0)
                for j in range(CONV_KERNEL):
                    off = halo - CONV_KERNEL // 2 + j
                    if off % 8 == s:
                        acc = acc + shifted[off - s:off - s + tm, :] * w_ref[j:j + 1, cs]
            cols.append(acc)
        y = jnp.concatenate(cols, axis=1) + b_ref[...]
        ms = jnp.mean(y * y, axis=-1, keepdims=True)
        y = y * lax.rsqrt(ms + EPS) * g_ref[...]
        r = pl.multiple_of(t * tm, tm)
        o_ref[pl.ds(r, tm), :] = (y * _sigmoid(y)).astype(o_ref.dtype)
        return carry

    lax.fori_loop(0, n_tiles, conv, 0)


def _conv(cu, conv_w, conv_b, conv_g, seq, n_tiles):
    b, nt, _ = cu.shape
    n_lat = seq // ROW_TILE
    return pl.pallas_call(
        functools.partial(_conv_kernel, n_lat=n_lat, n_tiles=n_tiles),
        grid=(b,),
        in_specs=[pl.BlockSpec((None, nt, 2 * MIX_WIDTH), lambda bb: (bb, 0, 0)),
                  _const_spec((CONV_KERNEL, MIX_WIDTH)),
                  _const_spec((1, MIX_WIDTH)), _const_spec((1, MIX_WIDTH))],
        out_specs=pl.BlockSpec((None, n_tiles * ROW_TILE, MIX_WIDTH), lambda bb: (bb, 0, 0)),
        out_shape=jax.ShapeDtypeStruct((b, n_tiles * ROW_TILE, MIX_WIDTH), BF16),
        scratch_shapes=[pltpu.VMEM((nt + 3 * CONV_HALO, MIX_WIDTH), F32)],
        compiler_params=_params(("parallel",)),
        name="conformer_conv",
    )(cu, conv_w, conv_b, conv_g)


_PAIRS = tuple((a, b) for a in range(EXPERTS_PER_GROUP) for b in range(a + 1, EXPERTS_PER_GROUP))
N_CLASSES = (N_EXPERTS // EXPERTS_PER_GROUP) * len(_PAIRS)


def _route(logits_t, bias):
    score = _sigmoid_exp(logits_t)
    sel = score + bias
    sel_r = [sel[e:e + 1] for e in range(N_EXPERTS)]
    n_groups = N_EXPERTS // EXPERTS_PER_GROUP

    def beats(a, ia, b, ib):
        return (a >= b) if ia < ib else (a > b)

    picked = []
    for g in range(n_groups):
        ids = range(g * EXPERTS_PER_GROUP, (g + 1) * EXPERTS_PER_GROUP)
        for e in ids:
            rank = sum(beats(sel_r[o], o, sel_r[e], e).astype(F32) for o in ids if o != e)
            picked.append(rank < 2.0)
    group_score = []
    for g in range(n_groups):
        ids = range(g * EXPERTS_PER_GROUP, (g + 1) * EXPERTS_PER_GROUP)
        group_score.append(sum(jnp.where(picked[e], sel_r[e], 0.0) for e in ids))
    cls = s_lo = s_hi = 0.0
    for g in range(n_groups):
        rank = sum(beats(group_score[o], o, group_score[g], g).astype(F32) for o in range(n_groups) if o != g)
        best = rank < 1.0
        for pid, (a, b) in enumerate(_PAIRS):
            ea, eb = g * EXPERTS_PER_GROUP + a, g * EXPERTS_PER_GROUP + b
            both = picked[ea] & picked[eb] & best
            cls = cls + jnp.where(both, float(g * len(_PAIRS) + pid), 0.0)
            s_lo = s_lo + jnp.where(both, score[ea:ea + 1], 0.0)
            s_hi = s_hi + jnp.where(both, score[eb:eb + 1], 0.0)
    denom = s_lo + s_hi
    return cls, s_lo / denom, s_hi / denom


def _merge_kernel(yg_ref, yn_ref, yf_ref, yfc_ref, yc_ref, gl_ref, xl_ref, xc_ref, mod_ref, g2_ref, wb_ref, wo_ref,
                  wr_ref, rb_ref, xo_ref, row_ref, cls_ref, *, n_lat):
    y_fourier = _stream_tile(yf_ref, yfc_ref, n_lat)
    acc = None
    for n, y in enumerate((yg_ref[...], yn_ref[...], y_fourier, yc_ref[...])):
        z = _dot(y, wb_ref[n])
        gate = _sigmoid(gl_ref[:, n * D_MODEL:(n + 1) * D_MODEL].astype(F32))
        acc = gate * z if acc is None else acc + gate * z
    mod = mod_ref[...]
    x = _stream_tile(xl_ref, xc_ref, n_lat) + mod[2:3] * _dot(acc.astype(BF16), wo_ref[...])
    xo_ref[...] = x
    h2 = _modulated_norm(x, g2_ref[...], mod[3:4], mod[4:5])
    logits_t = lax.dot_general(wr_ref[...], h2, (((1,), (1,)), ((), ())),
                               preferred_element_type=F32, precision=lax.Precision.HIGHEST)
    cls, w_lo, w_hi = _route(logits_t, rb_ref[...])
    tm = h2.shape[0]
    cls_ref[...] = jnp.concatenate([cls, jnp.zeros((7, tm), F32)], axis=0)
    w_t = jnp.concatenate([w_lo, w_hi, jnp.zeros((ROW_EXTRA - 2, tm), F32)], axis=0).T
    row_ref[...] = jnp.concatenate([h2, w_t], axis=1)


def _merge(ys, gl, stream, modtab, g2, w_branch, w_out, layer, w_router_t, router_bias, n_lat, n_tiles):
    x_lat, x_ctx, ctx_block = stream
    b, _, d = x_lat.shape
    tm = ROW_TILE
    row = lambda width: pl.BlockSpec((None, tm, width), lambda bb, i: (bb, i, 0))
    return pl.pallas_call(
        functools.partial(_merge_kernel, n_lat=n_lat),
        grid=(b, n_tiles),
        in_specs=[row(512), row(512)] + _stream_specs(tm, 512, n_lat, 0) + [row(512), row(4096)]
                 + _stream_specs(tm, d, n_lat, ctx_block) + [
                  pl.BlockSpec((None, None, 6, d), lambda bb, i: (bb, i // n_lat, 0, 0)),
                  _const_spec((1, d)),
                  _layer_spec((N_BRANCHES, MIX_WIDTH, d), layer),
                  _layer_spec((d, d), layer),
                  _const_spec((N_EXPERTS, d)),
                  _const_spec((N_EXPERTS, 1))],
        out_specs=[row(d), row(d + ROW_EXTRA),
                   pl.BlockSpec((None, None, 8, tm), lambda bb, i: (bb, i, 0, 0))],
        out_shape=[jax.ShapeDtypeStruct((b, n_tiles * tm, d), F32),
                   jax.ShapeDtypeStruct((b, n_tiles * tm, d + ROW_EXTRA), F32),
                   jax.ShapeDtypeStruct((b, n_tiles, 8, tm), F32)],
        compiler_params=_params(("parallel", "parallel")),
        name="merge_router",
    )(*ys, gl, x_lat, x_ctx, modtab, g2, w_branch, w_out, w_router_t, router_bias)


def _plan_kernel(cls_ref, tri_ref, pos_ref, meta_ref, *, n_chunks):
    width = cls_ref.shape[1]
    kk = lax.broadcasted_iota(jnp.int32, (32, width), 0).astype(F32)

    def onehot(c):
        return jnp.where(cls_ref[c:c + 1, :] == kk, 1.0, 0.0)

    cnt = jnp.zeros((32, 1), F32)
    for c in range(n_chunks):
        cnt = cnt + jnp.sum(onehot(c), axis=1, keepdims=True)
    size = jnp.floor((cnt + (EXPERT_TILE - 1)) * (1.0 / EXPERT_TILE)) * EXPERT_TILE
    starts = []
    acc = jnp.zeros((1, 1), F32)
    for k in range(32):
        starts.append(acc)
        acc = acc + size[k:k + 1]
    base = jnp.concatenate(starts, axis=0)
    run = base
    for c in range(n_chunks):
        oh = onehot(c)
        before = _dot(oh.astype(BF16), tri_ref[...])
        pos_ref[c:c + 1, :] = jnp.sum(oh * (run + before), axis=0, keepdims=True).astype(jnp.int32)
        run = run + jnp.sum(oh, axis=1, keepdims=True)

    end = base + size
    tile_row = lax.broadcasted_iota(jnp.int32, (1, 128), 1).astype(F32) * EXPERT_TILE
    tcls = jnp.sum(jnp.where(end[:N_CLASSES] <= tile_row, 1.0, 0.0), axis=0, keepdims=True)
    tcls = jnp.minimum(tcls, N_CLASSES - 1.0)
    n_pairs = float(len(_PAIRS))
    grp = sum(jnp.where(tcls >= n_pairs * g, 1.0, 0.0) for g in range(1, N_EXPERTS // EXPERTS_PER_GROUP))
    pid = tcls - n_pairs * grp
    lo = sum(jnp.where(pid == float(i), float(a), 0.0) for i, (a, _) in enumerate(_PAIRS))
    hi = sum(jnp.where(pid == float(i), float(b), 0.0) for i, (_, b) in enumerate(_PAIRS))
    used = jnp.broadcast_to(end[N_CLASSES - 1:N_CLASSES] * (1.0 / EXPERT_TILE), (1, 128))
    rows = [grp * EXPERTS_PER_GROUP + lo, grp * EXPERTS_PER_GROUP + hi, used, jnp.zeros((5, 128), F32)]
    meta_ref[...] = jnp.concatenate(rows, axis=0).astype(jnp.int32)


def _plan(cls):
    n_chunks, width = cls.shape
    tri = jnp.asarray(np.triu(np.ones((width, width), np.float32), k=1)).astype(BF16)
    return pl.pallas_call(
        functools.partial(_plan_kernel, n_chunks=n_chunks),
        out_shape=[jax.ShapeDtypeStruct((n_chunks, width), jnp.int32),
                   jax.ShapeDtypeStruct((8, 128), jnp.int32)],
        name="moe_plan",
    )(cls, tri)


def _row_copy(src_ref, src_row, dst_ref, dst_row, sem):
    return pltpu.make_async_copy(src_ref.at[pl.ds(src_row, 1)], dst_ref.at[pl.ds(dst_row, 1)], sem)


def _dispatch_kernel(pos_ref, src_ref, init_ref, dst_ref, sem):
    del init_ref
    n_sub, width = pos_ref.shape
    for q in range(n_sub):
        def issue(r, carry, q=q):
            _row_copy(src_ref, q * width + r, dst_ref, pos_ref[q, r], sem).start()
            return carry
        lax.fori_loop(0, width, issue, 0, unroll=8)

    def drain(r, carry):
        _row_copy(src_ref, 0, dst_ref, 0, sem).wait()
        return carry
    lax.fori_loop(0, n_sub * width, drain, 0, unroll=8)


def _dispatch(pos, rows, n_sorted):
    n_steps, n_sub, width = pos.shape
    t, w = rows.shape
    assert t == n_steps * n_sub * width
    return pl.pallas_call(
        _dispatch_kernel,
        grid=(n_steps,),
        in_specs=[pl.BlockSpec((None, n_sub, width), lambda i: (i, 0, 0), memory_space=pltpu.SMEM),
                  pl.BlockSpec((n_sub * width, w), lambda i: (i, 0)),
                  pl.BlockSpec(memory_space=pl.ANY)],
        out_specs=pl.BlockSpec(memory_space=pl.ANY),
        out_shape=jax.ShapeDtypeStruct((n_sorted, w), F32),
        scratch_shapes=[pltpu.SemaphoreType.DMA(())],
        input_output_aliases={2: 0},
        compiler_params=_params(("arbitrary",)),
        name="moe_dispatch",
    )(pos, rows, jnp.zeros((n_sorted, w), F32))


def _expert_pair_kernel(lo_ref, hi_ref, used_ref, xs_ref, gu_lo_ref, dn_lo_ref, gu_hi_ref, dn_hi_ref, o_ref):
    del lo_ref, hi_ref
    i = pl.program_id(0)

    @pl.when(i < used_ref[0])
    def _():
        x = xs_ref[:, :D_MODEL].astype(BF16)

        def ffn(gu_ref, dn_ref):
            ab = _dot(x, gu_ref[...])
            a = ab[:, :EXPERT_FF]
            return _dot((a * _sigmoid(a) * ab[:, EXPERT_FF:]).astype(BF16), dn_ref[...])

        o_ref[...] = (xs_ref[:, D_MODEL:D_MODEL + 1] * ffn(gu_lo_ref, dn_lo_ref)
                      + xs_ref[:, D_MODEL + 1:D_MODEL + 2] * ffn(gu_hi_ref, dn_hi_ref))

    @pl.when(i >= used_ref[0])
    def _():
        o_ref[...] = jnp.zeros_like(o_ref)


def _expert_pairs(meta, xs, w_gu, w_down, layer):
    n_sorted, w = xs.shape
    d = D_MODEL
    te = EXPERT_TILE
    gu = lambda ref: pl.BlockSpec((None, None, d, 2 * EXPERT_FF),
                                  lambda i, lo, hi, used: (layer, ref(lo, hi)[i], 0, 0))
    dn = lambda ref: pl.BlockSpec((None, None, EXPERT_FF, d),
                                  lambda i, lo, hi, used: (layer, ref(lo, hi)[i], 0, 0))
    first, second = (lambda lo, hi: lo), (lambda lo, hi: hi)
    grid_spec = pltpu.PrefetchScalarGridSpec(
        num_scalar_prefetch=3,
        grid=(n_sorted // te,),
        in_specs=[pl.BlockSpec((te, w), lambda i, lo, hi, used: (i, 0)),
                  gu(first), dn(first), gu(second), dn(second)],
        out_specs=pl.BlockSpec((te, d), lambda i, lo, hi, used: (i, 0)))
    return pl.pallas_call(
        _expert_pair_kernel,
        grid_spec=grid_spec,
        out_shape=jax.ShapeDtypeStruct((n_sorted, d), F32),
        compiler_params=_params(("arbitrary",)),
        name="moe_expert_pairs",
    )(meta[0], meta[1], meta[2, :1], xs, w_gu, w_down, w_gu, w_down)


def _combine_kernel(pos_ref, x_ref, g2_ref, *rest, final):
    if final:
        g_ref, ys_ref, o_ref, z_ref, sem = rest
    else:
        ys_ref, o_ref, z_ref, sem = rest
    n_sub, width = pos_ref.shape
    for q in range(n_sub):
        def issue(r, carry, q=q):
            _row_copy(ys_ref, pos_ref[q, r], z_ref, q * width + r, sem).start()
            return carry
        lax.fori_loop(0, width, issue, 0, unroll=8)

    def drain(r, carry):
        _row_copy(ys_ref, 0, z_ref, 0, sem).wait()
        return carry
    lax.fori_loop(0, n_sub * width, drain, 0, unroll=8)

    for q in range(n_sub):
        rows = slice(q * width, (q + 1) * width)
        x = x_ref[rows, :] + g2_ref[q] * z_ref[rows, :]
        if final:
            ms = jnp.mean(x * x, axis=-1, keepdims=True)
            x = x * lax.rsqrt(ms + EPS) * g_ref[...]
        o_ref[rows, :] = x


def _combine(pos, x, ys, g2, final_g=None):
    n_steps, n_sub, width = pos.shape
    t, d = x.shape
    tm = n_sub * width
    row = pl.BlockSpec((tm, d), lambda i: (i, 0))
    in_specs = [pl.BlockSpec((None, n_sub, width), lambda i: (i, 0, 0), memory_space=pltpu.SMEM),
                row,
                pl.BlockSpec((n_sub, 1, d), lambda i: (i, 0, 0))]
    args = [pos, x, g2]
    if final_g is not None:
        in_specs.append(_const_spec((1, d)))
        args.append(final_g)
    in_specs.append(pl.BlockSpec(memory_space=pl.ANY))
    args.append(ys)
    return pl.pallas_call(
        functools.partial(_combine_kernel, final=final_g is not None),
        grid=(n_steps,),
        in_specs=in_specs,
        out_specs=row,
        out_shape=jax.ShapeDtypeStruct((t, d), F32),
        scratch_shapes=[pltpu.VMEM((tm, d), F32), pltpu.SemaphoreType.DMA(())],
        compiler_params=_params(("arbitrary",)),
        name="moe_combine_final_norm" if final_g is not None else "moe_combine",
    )(*args)


def _moe(rows, cls, xa, modtab, w_gu, w_down, layer, n_lat, final_g=None):
    b, n, w = rows.shape
    d = xa.shape[-1]
    t = b * n
    n_tiles = n // ROW_TILE
    n_chunks = b * n_tiles
    n_sub = DISPATCH_CHUNKS if n_chunks % DISPATCH_CHUNKS == 0 else 1
    n_sorted = t + N_CLASSES * EXPERT_TILE
    assert n_sorted // EXPERT_TILE <= 128
    pos, meta = _plan(cls[:, :, 0, :].reshape(n_chunks, ROW_TILE))
    pos = pos.reshape(n_chunks // n_sub, n_sub, ROW_TILE)
    xs = _dispatch(pos, rows.reshape(t, w), n_sorted)
    ys = _expert_pairs(meta, xs, w_gu, w_down, layer)
    g2 = modtab[:, :, 5, :]
    g2 = jnp.concatenate([jnp.broadcast_to(g2[:, :1], (b, n_lat, d)), g2[:, 1:]], axis=1)[:, :n_tiles]
    out = _combine(pos, xa.reshape(t, d), ys, g2.reshape(n_chunks, 1, d), final_g)
    return out.reshape(b, n, d)


def _rope_tables(seq, ctx):
    t = jnp.arange(seq, dtype=jnp.int32)
    row = (t // GRID_W).astype(F32)
    col = (t % GRID_W).astype(F32)
    n_pairs = HEAD_DIM // 4
    inv_freq = ROPE_THETA ** (-jnp.arange(n_pairs, dtype=F32) / n_pairs)
    ang = jnp.concatenate([row[:, None] * inv_freq, col[:, None] * inv_freq], axis=-1)
    cos = jnp.repeat(jnp.cos(ang), 2, axis=-1)
    sin = jnp.repeat(jnp.sin(ang), 2, axis=-1) * jnp.tile(jnp.array([-1.0, 1.0], F32), HEAD_DIM // 2)
    cos = jnp.concatenate([cos, jnp.ones((ctx, HEAD_DIM), F32)], axis=0)
    sin = jnp.concatenate([sin, jnp.zeros((ctx, HEAD_DIM), F32)], axis=0)
    ck, sk = jnp.tile(cos, (1, GQA_KV_HEADS)), jnp.tile(sin, (1, GQA_KV_HEADS))
    cq, sq = jnp.tile(cos, (1, N_HEADS)) * Q_SCALE, jnp.tile(sin, (1, N_HEADS)) * Q_SCALE
    return cq, sq, ck, sk


def _group_sum_matrix():
    idx = np.arange(MIX_WIDTH) // HEAD_DIM
    return jnp.asarray(idx[:, None] == idx[None, :], dtype=BF16)


def _channel_dft_matrix():
    c = np.arange(FOURIER_GROUP_DIM)
    ang = 2.0 * np.pi * ((c[:, None] * c[None, :]) % FOURIER_GROUP_DIM) / FOURIER_GROUP_DIM
    n_groups = MIX_WIDTH // FOURIER_GROUP_DIM
    eye = np.eye(n_groups)
    m = np.concatenate([np.kron(eye, np.cos(ang)), np.kron(eye, np.sin(ang))], axis=1)
    return jnp.asarray(m, dtype=F32).astype(BF16)


def kernel(x, c, ctx, c_ctx, w_mod, b_mod, norm1_g, norm2_g, w_in, q_norm_g, k_norm_g, na_rpb, conv_w, conv_b,
           conv_norm_g, w_branch, w_out, w_router, router_bias, w_expert_gu, w_expert_down, final_norm_g):
    b, seq, d = x.shape
    n_ctx = ctx.shape[1]
    depth = w_mod.shape[0]
    assert d == D_MODEL and seq % ROW_TILE == 0 and n_ctx == ROW_TILE and seq % n_ctx == 0
    assert seq // GRID_W >= NA_BAND_ROWS and b <= 7
    n_lat = seq // ROW_TILE
    n_all = n_lat + 1

    c_all = jnp.zeros((8, d), F32).at[:b].set(c).at[b].set(c_ctx)
    m = _modulation(c_all, w_mod, b_mod)
    m_lat = m[:, :b].reshape(depth, b, 1, 6, d)
    m_ctx = jnp.broadcast_to(m[:, b].reshape(depth, 1, 1, 6, d), (depth, b, 1, 6, d))
    modtab = jnp.concatenate([m_lat, m_ctx], axis=2)

    tabs = _rope_tables(seq, n_ctx)
    gsum = _group_sum_matrix()
    dcs = _channel_dft_matrix()
    dft_lat = _dft_mats(seq)
    dft_ctx = _dft_mats(n_ctx)
    w_router_t = w_router.T
    rb = router_bias.reshape(N_EXPERTS, 1)

    w_in_b, w_branch_b, w_out_b = w_in.astype(BF16), w_branch.astype(BF16), w_out.astype(BF16)
    w_gu_b, w_down_b = w_expert_gu.astype(BF16), w_expert_down.astype(BF16)
    na_bias = _na_bias(na_rpb, seq)

    stream = (x, ctx, 0)
    xa = None
    for l in range(depth):
        last = l == depth - 1
        n_tiles = n_lat if last else n_all
        qg = jnp.tile(q_norm_g[l], N_HEADS).reshape(1, -1)
        kg = jnp.tile(k_norm_g[l], GQA_KV_HEADS).reshape(1, -1)
        q, k, v, nq, nk, nv, fab, cu, gl = _inproj(
            stream, modtab[l], norm1_g[l].reshape(1, d), w_in_b, l, tabs, (qg, kg, gsum, dcs), n_lat)
        y_gqa = _gqa(q, k, v, seq, n_ctx, n_tiles)
        y_na = _na(nq, nk, nv, na_bias, l, seq, n_ctx, n_tiles)
        y_fn = _fourier_latent(fab, seq, dft_lat)
        y_fn_ctx = y_fn if last else _fourier_ctx(fab, seq, n_ctx, dft_ctx)
        y_cv = _conv(cu, conv_w[l], conv_b[l].reshape(1, -1), conv_norm_g[l].reshape(1, -1), seq, n_tiles)
        xa, rows, cls = _merge((y_gqa, y_na, y_fn, y_fn_ctx, y_cv), gl, stream, modtab[l], norm2_g[l].reshape(1, d),
                               w_branch_b, w_out_b, l, w_router_t, rb, n_lat, n_tiles)
        xa = _moe(rows, cls, xa, modtab[l], w_gu_b, w_down_b, l, n_lat,
                  final_norm_g.reshape(1, d) if last else None)
        stream = (xa, xa, n_lat)
    return xa
```

```python
import functools

import numpy as np
import jax
import jax.numpy as jnp
from jax import lax
from jax.experimental import pallas as pl
from jax.experimental.pallas import tpu as pltpu

D_MODEL = 1024
GRID_W = 64
MIX_WIDTH = 512
HEAD_DIM = 64
N_HEADS = 8
GQA_KV_HEADS = 2
NA_WIN_ROWS = 8
NA_WIN_COLS = 16
FOURIER_GROUP_DIM = 128
CONV_KERNEL = 31
N_BRANCHES = 4
N_EXPERTS = 16
EXPERTS_PER_GROUP = 4
EXPERT_FF = 512
ROPE_THETA = 10000.0
EPS = 1e-6
ATTN_SCALE = HEAD_DIM ** -0.5
LOG2E = float(np.log2(np.e))
Q_SCALE = ATTN_SCALE * LOG2E

_OFF_GQ, _OFF_GK, _OFF_GV = 0, 512, 640
_OFF_NQ, _OFF_NK, _OFF_NV = 768, 1280, 1792
_OFF_FU, _OFF_CU, _OFF_GL = 2304, 2816, 3840
IN_WIDTH = 7936

ROW_TILE = 256
NA_BLOCK_ROWS = ROW_TILE // GRID_W
NA_BAND_ROWS = NA_BLOCK_ROWS + NA_WIN_ROWS
ROW_EXTRA = 128
EXPERT_TILE = 256
DISPATCH_CHUNKS = 4
CONV_HALO = 16
MASK_VALUE = -1e30
VMEM_LIMIT = 56 * 1024 * 1024

F32 = jnp.float32
BF16 = jnp.bfloat16


def _params(sem, vmem=VMEM_LIMIT):
    return pltpu.CompilerParams(dimension_semantics=sem, vmem_limit_bytes=vmem)


def _const_spec(shape):
    nd = len(shape)
    return pl.BlockSpec(shape, lambda *_: (0,) * nd, pipeline_mode=pl.Buffered(1))


def _layer_spec(shape, layer):
    nd = len(shape)
    return pl.BlockSpec((None,) + tuple(shape), lambda *_: (layer,) + (0,) * nd, pipeline_mode=pl.Buffered(1))


def _stream_specs(tm, width, n_lat, ctx_block):
    return [pl.BlockSpec((None, tm, width), lambda bb, i: (bb, jnp.minimum(i, n_lat - 1), 0)),
            pl.BlockSpec((None, tm, width), lambda bb, i: (bb, ctx_block, 0))]


def _stream_tile(lat_ref, ctx_ref, n_lat):
    return jnp.where(pl.program_id(1) < n_lat, lat_ref[...], ctx_ref[...])


def _sigmoid(v):
    return 0.5 * jnp.tanh(0.5 * v) + 0.5


def _sigmoid_exp(v):
    return 1.0 / (1.0 + jnp.exp(-v))


def _dot(a, b):
    return jnp.dot(a, b, preferred_element_type=F32)


def _dot_t(a, b):
    return lax.dot_general(a, b, (((1,), (1,)), ((), ())), preferred_element_type=F32)


def _mod_kernel(c_ref, w_ref, b_ref, o_ref):
    cc = c_ref[...]
    a = cc * _sigmoid(cc)
    o_ref[...] = jnp.dot(a, w_ref[...], preferred_element_type=F32,
                         precision=lax.Precision.HIGHEST) + b_ref[...]


def _modulation(c_all, w_mod, b_mod):
    depth, d, n = w_mod.shape
    tn = 1536
    return pl.pallas_call(
        _mod_kernel,
        grid=(depth, n // tn),
        in_specs=[pl.BlockSpec((8, d), lambda l, j: (0, 0)),
                  pl.BlockSpec((None, d, tn), lambda l, j: (l, 0, j)),
                  pl.BlockSpec((None, 1, tn), lambda l, j: (l, 0, j))],
        out_specs=pl.BlockSpec((None, 8, tn), lambda l, j: (l, 0, j)),
        out_shape=jax.ShapeDtypeStruct((depth, 8, n), F32),
        compiler_params=_params(("arbitrary", "arbitrary")),
        name="modulation",
    )(c_all, w_mod, b_mod.reshape(depth, 1, n))


def _modulated_norm(x, g, shift, scale):
    ms = jnp.mean(x * x, axis=-1, keepdims=True)
    return x * lax.rsqrt(ms + EPS) * g * (1.0 + scale) + shift


def _head_norm(q, gsum, gain):
    sq = q * q
    hi = sq.astype(BF16)
    lo = (sq - hi.astype(F32)).astype(BF16)
    ss = (_dot(hi, gsum) + _dot(lo, gsum)) * (1.0 / HEAD_DIM)
    return q * lax.rsqrt(ss + EPS) * gain


def _rope(q, cos, sin_signed):
    n = q.shape[-1]
    lane = lax.broadcasted_iota(jnp.int32, q.shape, 1)
    swapped = jnp.where(lane % 2 == 0, pltpu.roll(q, n - 1, 1), pltpu.roll(q, 1, 1))
    return q * cos + swapped * sin_signed


def _inproj_kernel(xl_ref, xc_ref, mod_ref, g_ref, w_ref, cq_ref, sq_ref, ck_ref, sk_ref, qg_ref, kg_ref,
                   gsum_ref, dcs_ref,
                   q_ref, k_ref, v_ref, nq_ref, nk_ref, nv_ref, fab_ref, cu_ref, gl_ref, *, n_lat):
    mod = mod_ref[...]
    h = _modulated_norm(_stream_tile(xl_ref, xc_ref, n_lat), g_ref[...], mod[0:1], mod[1:2]).astype(BF16)

    def seg(off, width):
        return _dot(h, w_ref[:, off:off + width])

    gsum = gsum_ref[...]
    q = _head_norm(seg(_OFF_GQ, 512), gsum, qg_ref[...])
    q_ref[...] = _rope(q, cq_ref[...], sq_ref[...]).astype(BF16)
    kv = seg(_OFF_GK, 256)
    k = _head_norm(kv[:, :128], gsum[:128, :128], kg_ref[...])
    k_ref[...] = _rope(k, ck_ref[...], sk_ref[...]).astype(BF16)
    v_ref[...] = kv[:, 128:].astype(BF16)
    nq_ref[...] = (seg(_OFF_NQ, 512) * Q_SCALE).astype(BF16)
    nk_ref[...] = seg(_OFF_NK, 512).astype(BF16)
    nv_ref[...] = seg(_OFF_NV, 512).astype(BF16)
    ab = _dot(seg(_OFF_FU, 512).astype(BF16), dcs_ref[...])
    fab_ref[0] = ab[:, :512].astype(BF16)
    fab_ref[1] = ab[:, 512:].astype(BF16)
    for j in range(2):
        cu_ref[:, j * 512:(j + 1) * 512] = seg(_OFF_CU + j * 512, 512).astype(BF16)
    for j in range(4):
        gl_ref[:, j * 1024:(j + 1) * 1024] = (seg(_OFF_GL + j * 1024, 1024) * 0.5).astype(BF16)


def _inproj(stream, modtab, g1, w_in, layer, tabs, consts, n_lat):
    x_lat, x_ctx, ctx_block = stream
    b, _, d = x_lat.shape
    tm = ROW_TILE
    nt = (n_lat + 1) * tm
    cq, sq, ck, sk = tabs
    qg, kg, gsum, dcs = consts
    row = lambda width: pl.BlockSpec((None, tm, width), lambda bb, i: (bb, i, 0))
    tab = lambda width: pl.BlockSpec((tm, width), lambda bb, i: (i, 0))
    out_shapes = [jax.ShapeDtypeStruct((b, nt, wd), BF16) for wd in (512, 128, 128, 512, 512, 512)]
    out_shapes += [jax.ShapeDtypeStruct((b, 2, nt, 512), BF16),
                   jax.ShapeDtypeStruct((b, nt, 1024), BF16),
                   jax.ShapeDtypeStruct((b, nt, 4096), BF16)]
    out_specs = [row(512), row(128), row(128), row(512), row(512), row(512),
                 pl.BlockSpec((None, 2, tm, 512), lambda bb, i: (bb, 0, i, 0)),
                 row(1024), row(4096)]
    return pl.pallas_call(
        functools.partial(_inproj_kernel, n_lat=n_lat),
        grid=(b, nt // tm),
        in_specs=_stream_specs(tm, d, n_lat, ctx_block) + [
                  pl.BlockSpec((None, None, 6, d), lambda bb, i: (bb, i // n_lat, 0, 0)),
                  _const_spec((1, d)),
                  _layer_spec((d, IN_WIDTH), layer),
                  tab(512), tab(512), tab(128), tab(128),
                  _const_spec((1, 512)), _const_spec((1, 128)),
                  _const_spec((512, 512)), _const_spec((512, 1024))],
        out_specs=out_specs,
        out_shape=out_shapes,
        compiler_params=_params(("parallel", "parallel")),
        name="inproj",
    )(x_lat, x_ctx, modtab, g1, w_in, cq, sq, ck, sk, qg, kg, gsum, dcs)


def _with_ones(v):
    return jnp.concatenate([v, jnp.ones_like(v)], axis=1)


def _softmax_pv(parts):
    m = None
    for s, _ in parts:
        mi = jnp.max(s, axis=-1, keepdims=True)
        m = mi if m is None else jnp.maximum(m, mi)
    acc = None
    for s, v in parts:
        oi = _dot(jnp.exp2(s - m).astype(BF16), v)
        acc = oi if acc is None else acc + oi
    return acc[:, :HEAD_DIM] / acc[:, HEAD_DIM:HEAD_DIM + 1]


def _gqa_kernel(q_ref, k_ref, v_ref, o_ref, *, seq, ctx, n_lat):
    i = pl.program_id(1)
    group = N_HEADS // GQA_KV_HEADS

    def attend(lo, n):
        for kv in range(GQA_KV_HEADS):
            c0 = kv * HEAD_DIM
            kk = k_ref[lo:lo + n, c0:c0 + HEAD_DIM]
            vv = _with_ones(v_ref[lo:lo + n, c0:c0 + HEAD_DIM])
            for h in range(kv * group, (kv + 1) * group):
                qh = q_ref[:, h * HEAD_DIM:(h + 1) * HEAD_DIM]
                o = _softmax_pv([(_dot_t(qh, kk), vv)])
                o_ref[:, h * HEAD_DIM:(h + 1) * HEAD_DIM] = o.astype(o_ref.dtype)

    @pl.when(i < n_lat)
    def _():
        attend(0, seq + ctx)

    @pl.when(i >= n_lat)
    def _():
        attend(seq, ctx)


def _gqa(q, k, v, seq, ctx, n_tiles):
    b, nt, _ = q.shape
    tm = ROW_TILE
    n_lat = seq // tm
    return pl.pallas_call(
        functools.partial(_gqa_kernel, seq=seq, ctx=ctx, n_lat=n_lat),
        grid=(b, n_tiles),
        in_specs=[pl.BlockSpec((None, tm, 512), lambda bb, i: (bb, i, 0)),
                  pl.BlockSpec((None, nt, 128), lambda bb, i: (bb, 0, 0)),
                  pl.BlockSpec((None, nt, 128), lambda bb, i: (bb, 0, 0))],
        out_specs=pl.BlockSpec((None, tm, 512), lambda bb, i: (bb, i, 0)),
        out_shape=jax.ShapeDtypeStruct((b, n_tiles * tm, 512), BF16),
        compiler_params=_params(("parallel", "parallel")),
        name="gqa_attention",
    )(q, k, v)


def _na_kernel(q_ref, k_ref, v_ref, bias_ref, o_ref, *, seq, ctx, n_lat):
    i = pl.program_id(1)
    band = NA_BAND_ROWS * GRID_W
    rows = seq // GRID_W

    @pl.when(i < n_lat)
    def _():
        band_row = jnp.clip(i * NA_BLOCK_ROWS - NA_WIN_ROWS // 2, 0, rows - NA_BAND_ROWS)
        start = pl.multiple_of(band_row * GRID_W, ROW_TILE)
        for h in range(N_HEADS):
            hs = slice(h * HEAD_DIM, (h + 1) * HEAD_DIM)
            qh = q_ref[:, hs]
            s_band = _dot_t(qh, k_ref[pl.ds(start, band), hs]) + bias_ref[h]
            s_ctx = _dot_t(qh, k_ref[seq:seq + ctx, hs])
            o = _softmax_pv([(s_band, _with_ones(v_ref[pl.ds(start, band), hs])),
                             (s_ctx, _with_ones(v_ref[seq:seq + ctx, hs]))])
            o_ref[:, hs] = o.astype(o_ref.dtype)

    @pl.when(i >= n_lat)
    def _():
        for h in range(N_HEADS):
            hs = slice(h * HEAD_DIM, (h + 1) * HEAD_DIM)
            o = _softmax_pv([(_dot_t(q_ref[:, hs], k_ref[seq:seq + ctx, hs]), _with_ones(v_ref[seq:seq + ctx, hs]))])
            o_ref[:, hs] = o.astype(o_ref.dtype)


def _na(q, k, v, bias, layer, seq, ctx, n_tiles):
    b, nt, _ = q.shape
    tm = ROW_TILE
    n_lat = seq // tm
    band = NA_BAND_ROWS * GRID_W

    def bias_idx(bb, i):
        return (jnp.where(i == 0, 0, jnp.where(i >= n_lat - 1, 2, 1)), layer, 0, 0, 0)

    return pl.pallas_call(
        functools.partial(_na_kernel, seq=seq, ctx=ctx, n_lat=n_lat),
        grid=(b, n_tiles),
        in_specs=[pl.BlockSpec((None, tm, 512), lambda bb, i: (bb, i, 0)),
                  pl.BlockSpec((None, nt, 512), lambda bb, i: (bb, 0, 0)),
                  pl.BlockSpec((None, nt, 512), lambda bb, i: (bb, 0, 0)),
                  pl.BlockSpec((None, None, N_HEADS, tm, band), bias_idx)],
        out_specs=pl.BlockSpec((None, tm, 512), lambda bb, i: (bb, i, 0)),
        out_shape=jax.ShapeDtypeStruct((b, n_tiles * tm, 512), BF16),
        compiler_params=_params(("parallel", "parallel")),
        name="neighbourhood_attention",
    )(q, k, v, bias)


def _bias_rows_kernel(rpb_ref, onehot_ref, mask_ref, o_ref):
    o_ref[...] = jnp.dot(rpb_ref[...], onehot_ref[...], preferred_element_type=F32,
                         precision=lax.Precision.HIGHEST) + mask_ref[...]


def _na_bias(rpb, seq):
    n_dr, n_dc = 2 * NA_WIN_ROWS - 1, 2 * NA_WIN_COLS - 1
    qc = np.arange(GRID_W)
    c0 = np.clip(qc - NA_WIN_COLS // 2, 0, GRID_W - NA_WIN_COLS)
    col_ok = (qc[None, :] >= c0[:, None]) & (qc[None, :] < c0[:, None] + NA_WIN_COLS)
    col_idx = qc[None, :] - qc[:, None] + NA_WIN_COLS - 1
    onehot = np.zeros((128, GRID_W * GRID_W), np.float32)
    for j in range(n_dc):
        onehot[j] = ((col_idx == j) & col_ok).reshape(-1)
    mask = np.where(col_ok, 0.0, MASK_VALUE).astype(np.float32).reshape(1, -1)
    n_heads = rpb.shape[0] * rpb.shape[1]
    n_rows = -(-n_heads * n_dr // 128) * 128
    rpb_rows = jnp.zeros((n_rows, 128), F32).at[:n_heads * n_dr, :n_dc].set(rpb.astype(F32).reshape(-1, n_dc))
    blocks = pl.pallas_call(
        _bias_rows_kernel,
        out_shape=jax.ShapeDtypeStruct((n_rows, GRID_W * GRID_W), F32),
        name="na_bias_rows",
    )(rpb_rows, jnp.asarray(onehot), jnp.asarray(mask))
    blocks = (blocks[:n_heads * n_dr] * LOG2E).reshape(n_heads, n_dr, GRID_W, GRID_W)
    masked = jnp.full((n_heads, GRID_W, GRID_W), MASK_VALUE, F32)

    rows = seq // GRID_W
    n_blk = rows // NA_BLOCK_ROWS
    tables = []
    for blk in (0, 1, n_blk - 1):
        band_row = int(np.clip(blk * NA_BLOCK_ROWS - NA_WIN_ROWS // 2, 0, rows - NA_BAND_ROWS))
        strips = []
        for qr in range(blk * NA_BLOCK_ROWS, (blk + 1) * NA_BLOCK_ROWS):
            r0 = int(np.clip(qr - NA_WIN_ROWS // 2, 0, rows - NA_WIN_ROWS))
            strip = [blocks[:, kr - qr + NA_WIN_ROWS - 1] if r0 <= kr < r0 + NA_WIN_ROWS else masked
                     for kr in range(band_row, band_row + NA_BAND_ROWS)]
            strips.append(jnp.concatenate(strip, axis=-1))
        tables.append(jnp.concatenate(strips, axis=1))
    return jnp.stack(tables).reshape(3, rpb.shape[0], rpb.shape[1], ROW_TILE, NA_BAND_ROWS * GRID_W)


def _dft_kernel(wc_ref, ws_ref, a_ref, b_ref, o_ref):
    y = _dot(wc_ref[...], a_ref[...]) - _dot(ws_ref[...], b_ref[...])
    o_ref[...] = y.astype(o_ref.dtype)


def _dft_mats(n):
    r = int(round(np.sqrt(n)))
    assert r * r == n
    k = jnp.arange(n, dtype=jnp.int32)[None, :]
    j = jnp.arange(r, dtype=jnp.int32)[:, None]
    a = ((k * j) % r).astype(F32) * (2.0 * np.pi / r)
    bb = ((k * j) % n).astype(F32) * (2.0 * np.pi / n)
    ca, sa = jnp.cos(a)[:, None, :], jnp.sin(a)[:, None, :]
    cb, sb = jnp.cos(bb)[None, :, :], jnp.sin(bb)[None, :, :]
    scale = 1.0 / np.sqrt(n * FOURIER_GROUP_DIM)
    cos = ((ca * cb - sa * sb) * scale).reshape(n, n)
    sin = ((sa * cb + ca * sb) * scale).reshape(n, n)
    return cos.astype(BF16), sin.astype(BF16)


def _fourier_latent(fab, seq, mats):
    b, _, nt, w = fab.shape
    tf = 512 if seq % 512 == 0 else ROW_TILE
    wc, ws = mats
    return pl.pallas_call(
        _dft_kernel,
        grid=(seq // tf, b),
        in_specs=[pl.BlockSpec((tf, seq), lambda i, bb: (i, 0)),
                  pl.BlockSpec((tf, seq), lambda i, bb: (i, 0)),
                  pl.BlockSpec((None, None, seq, w), lambda i, bb: (bb, 0, 0, 0)),
                  pl.BlockSpec((None, None, seq, w), lambda i, bb: (bb, 1, 0, 0))],
        out_specs=pl.BlockSpec((None, tf, w), lambda i, bb: (bb, i, 0)),
        out_shape=jax.ShapeDtypeStruct((b, seq, w), BF16),
        compiler_params=_params(("parallel", "parallel")),
        name="fourier_latent",
    )(wc, ws, fab, fab)


def _fourier_ctx(fab, seq, ctx, mats):
    b, _, nt, w = fab.shape
    blk = seq // ctx
    wc, ws = mats
    return pl.pallas_call(
        _dft_kernel,
        grid=(b,),
        in_specs=[_const_spec((ctx, ctx)), _const_spec((ctx, ctx)),
                  pl.BlockSpec((None, None, ctx, w), lambda bb: (bb, 0, blk, 0)),
                  pl.BlockSpec((None, None, ctx, w), lambda bb: (bb, 1, blk, 0))],
        out_specs=pl.BlockSpec((None, ctx, w), lambda bb: (bb, 0, 0)),
        out_shape=jax.ShapeDtypeStruct((b, ctx, w), BF16),
        compiler_params=_params(("parallel",)),
        name="fourier_ctx",
    )(wc, ws, fab, fab)


def _conv_kernel(cu_ref, w_ref, b_ref, g_ref, o_ref, z_ref, *, n_lat, n_tiles):
    tm = ROW_TILE
    halo = CONV_HALO
    zeros = jnp.zeros((halo, MIX_WIDTH), F32)
    lat_end = halo + n_lat * tm
    z_ref[0:halo, :] = zeros
    z_ref[lat_end:lat_end + halo, :] = zeros
    ctx_end = lat_end + halo + (n_tiles - n_lat) * tm
    z_ref[ctx_end:ctx_end + halo, :] = zeros

    def tile_base(t):
        return pl.multiple_of(t * tm + halo + jnp.where(t >= n_lat, halo, 0), 8)

    def glu(t, carry):
        r = pl.multiple_of(t * tm, tm)
        a = cu_ref[pl.ds(r, tm), 0:MIX_WIDTH].astype(F32)
        g = cu_ref[pl.ds(r, tm), MIX_WIDTH:2 * MIX_WIDTH].astype(F32)
        z_ref[pl.ds(tile_base(t), tm), :] = a * _sigmoid(g)
        return carry

    lax.fori_loop(0, n_tiles, glu, 0)

    def conv(t, carry):
        start = pl.multiple_of(tile_base(t) - halo, 8)
        cols = []
        for c in range(MIX_WIDTH // 128):
            cs = slice(c * 128, (c + 1) * 128)
            win = z_ref[pl.ds(start, tm + 2 * halo), cs]
            acc = jnp.zeros((tm, 128), F32)
            n_win = tm + 2 * halo
            for s in range(8):
                shifted = win if s == 0 else pltpu.roll(win, n_win - s, 0)
                for j in range(CONV_KERNEL):
                    off = halo - CONV_KERNEL // 2 + j
                    if off % 8 == s:
                        acc = acc + shifted[off - s:off - s + tm, :] * w_ref[j:j + 1, cs]
            cols.append(acc)
        y = jnp.concatenate(cols, axis=1) + b_ref[...]
        ms = jnp.mean(y * y, axis=-1, keepdims=True)
        y = y * lax.rsqrt(ms + EPS) * g_ref[...]
        r = pl.multiple_of(t * tm, tm)
        o_ref[pl.ds(r, tm), :] = (y * _sigmoid(y)).astype(o_ref.dtype)
        return carry

    lax.fori_loop(0, n_tiles, conv, 0)


def _conv(cu, conv_w, conv_b, conv_g, seq, n_tiles):
    b, nt, _ = cu.shape
    n_lat = seq // ROW_TILE
    return pl.pallas_call(
        functools.partial(_conv_kernel, n_lat=n_lat, n_tiles=n_tiles),
        grid=(b,),
        in_specs=[pl.BlockSpec((None, nt, 2 * MIX_WIDTH), lambda bb: (bb, 0, 0)),
                  _const_spec((CONV_KERNEL, MIX_WIDTH)),
                  _const_spec((1, MIX_WIDTH)), _const_spec((1, MIX_WIDTH))],
        out_specs=pl.BlockSpec((None, n_tiles * ROW_TILE, MIX_WIDTH), lambda bb: (bb, 0, 0)),
        out_shape=jax.ShapeDtypeStruct((b, n_tiles * ROW_TILE, MIX_WIDTH), BF16),
        scratch_shapes=[pltpu.VMEM((nt + 3 * CONV_HALO, MIX_WIDTH), F32)],
        compiler_params=_params(("parallel",)),
        name="conformer_conv",
    )(cu, conv_w, conv_b, conv_g)


_PAIRS = tuple((a, b) for a in range(EXPERTS_PER_GROUP) for b in range(a + 1, EXPERTS_PER_GROUP))
N_CLASSES = (N_EXPERTS // EXPERTS_PER_GROUP) * len(_PAIRS)


def _route(logits_t, bias):
    score = _sigmoid_exp(logits_t)
    sel = score + bias
    sel_r = [sel[e:e + 1] for e in range(N_EXPERTS)]
    n_groups = N_EXPERTS // EXPERTS_PER_GROUP

    def beats(a, ia, b, ib):
        return (a >= b) if ia < ib else (a > b)

    picked = []
    for g in range(n_groups):
        ids = range(g * EXPERTS_PER_GROUP, (g + 1) * EXPERTS_PER_GROUP)
        for e in ids:
            rank = sum(beats(sel_r[o], o, sel_r[e], e).astype(F32) for o in ids if o != e)
            picked.append(rank < 2.0)
    group_score = []
    for g in range(n_groups):
        ids = range(g * EXPERTS_PER_GROUP, (g + 1) * EXPERTS_PER_GROUP)
        group_score.append(sum(jnp.where(picked[e], sel_r[e], 0.0) for e in ids))
    cls = s_lo = s_hi = 0.0
    for g in range(n_groups):
        rank = sum(beats(group_score[o], o, group_score[g], g).astype(F32) for o in range(n_groups) if o != g)
        best = rank < 1.0
        for pid, (a, b) in enumerate(_PAIRS):
            ea, eb = g * EXPERTS_PER_GROUP + a, g * EXPERTS_PER_GROUP + b
            both = picked[ea] & picked[eb] & best
            cls = cls + jnp.where(both, float(g * len(_PAIRS) + pid), 0.0)
            s_lo = s_lo + jnp.where(both, score[ea:ea + 1], 0.0)
            s_hi = s_hi + jnp.where(both, score[eb:eb + 1], 0.0)
    denom = s_lo + s_hi
    return cls, s_lo / denom, s_hi / denom


def _merge_kernel(yg_ref, yn_ref, yf_ref, yfc_ref, yc_ref, gl_ref, xl_ref, xc_ref, mod_ref, g2_ref, wb_ref, wo_ref,
                  wr_ref, rb_ref, xo_ref, row_ref, cls_ref, *, n_lat):
    is_latent = pl.program_id(1) < n_lat
    mod = mod_ref[...]
    half = xo_ref.shape[0] // 2
    for r0 in (0, half):
        rows = slice(r0, r0 + half)
        y_fourier = jnp.where(is_latent, yf_ref[rows, :], yfc_ref[rows, :])
        acc = None
        for n, y in enumerate((yg_ref[rows, :], yn_ref[rows, :], y_fourier, yc_ref[rows, :])):
            z = _dot(y, wb_ref[n])
            gate2 = 1.0 + jnp.tanh(gl_ref[rows, n * D_MODEL:(n + 1) * D_MODEL].astype(F32))
            acc = gate2 * z if acc is None else acc + gate2 * z
        x = (jnp.where(is_latent, xl_ref[rows, :], xc_ref[rows, :])
             + (0.5 * mod[2:3]) * _dot(acc.astype(BF16), wo_ref[...]))
        xo_ref[rows, :] = x
        h2 = _modulated_norm(x, g2_ref[...], mod[3:4], mod[4:5])
        logits_t = lax.dot_general(wr_ref[...], h2, (((1,), (1,)), ((), ())),
                                   preferred_element_type=F32, precision=lax.Precision.HIGHEST)
        cls, w_lo, w_hi = _route(logits_t, rb_ref[...])
        cls_ref[:, rows] = jnp.concatenate([cls, jnp.zeros((7, half), F32)], axis=0)
        w_t = jnp.concatenate([w_lo, w_hi, jnp.zeros((ROW_EXTRA - 2, half), F32)], axis=0).T
        row_ref[rows, :] = jnp.concatenate([h2, w_t], axis=1)


def _merge(ys, gl, stream, modtab, g2, w_branch, w_out, layer, w_router_t, router_bias, n_lat, n_tiles):
    x_lat, x_ctx, ctx_block = stream
    b, _, d = x_lat.shape
    tm = ROW_TILE
    row = lambda width: pl.BlockSpec((None, tm, width), lambda bb, i: (bb, i, 0))
    return pl.pallas_call(
        functools.partial(_merge_kernel, n_lat=n_lat),
        grid=(b, n_tiles),
        in_specs=[row(512), row(512)] + _stream_specs(tm, 512, n_lat, 0) + [row(512), row(4096)]
                 + _stream_specs(tm, d, n_lat, ctx_block) + [
                  pl.BlockSpec((None, None, 6, d), lambda bb, i: (bb, i // n_lat, 0, 0)),
                  _const_spec((1, d)),
                  _layer_spec((N_BRANCHES, MIX_WIDTH, d), layer),
                  _layer_spec((d, d), layer),
                  _const_spec((N_EXPERTS, d)),
                  _const_spec((N_EXPERTS, 1))],
        out_specs=[row(d), row(d + ROW_EXTRA),
                   pl.BlockSpec((None, None, 8, tm), lambda bb, i: (bb, i, 0, 0))],
        out_shape=[jax.ShapeDtypeStruct((b, n_tiles * tm, d), F32),
                   jax.ShapeDtypeStruct((b, n_tiles * tm, d + ROW_EXTRA), F32),
                   jax.ShapeDtypeStruct((b, n_tiles, 8, tm), F32)],
        compiler_params=_params(("parallel", "parallel")),
        name="merge_router",
    )(*ys, gl, x_lat, x_ctx, modtab, g2, w_branch, w_out, w_router_t, router_bias)


def _plan_kernel(cls_ref, tri_ref, pos_ref, meta_ref, *, n_chunks):
    width = cls_ref.shape[1]
    kk = lax.broadcasted_iota(jnp.int32, (32, width), 0).astype(F32)

    def onehot(c):
        return jnp.where(cls_ref[c:c + 1, :] == kk, 1.0, 0.0)

    cnt = jnp.zeros((32, 1), F32)
    for c in range(n_chunks):
        cnt = cnt + jnp.sum(onehot(c), axis=1, keepdims=True)
    size = jnp.floor((cnt + (EXPERT_TILE - 1)) * (1.0 / EXPERT_TILE)) * EXPERT_TILE
    starts = []
    acc = jnp.zeros((1, 1), F32)
    for k in range(32):
        starts.append(acc)
        acc = acc + size[k:k + 1]
    base = jnp.concatenate(starts, axis=0)
    run = base
    for c in range(n_chunks):
        oh = onehot(c)
        before = _dot(oh.astype(BF16), tri_ref[...])
        pos_ref[c:c + 1, :] = jnp.sum(oh * (run + before), axis=0, keepdims=True).astype(jnp.int32)
        run = run + jnp.sum(oh, axis=1, keepdims=True)

    end = base + size
    tile_row = lax.broadcasted_iota(jnp.int32, (1, 128), 1).astype(F32) * EXPERT_TILE
    tcls = jnp.sum(jnp.where(end[:N_CLASSES] <= tile_row, 1.0, 0.0), axis=0, keepdims=True)
    tcls = jnp.minimum(tcls, N_CLASSES - 1.0)
    n_pairs = float(len(_PAIRS))
    grp = sum(jnp.where(tcls >= n_pairs * g, 1.0, 0.0) for g in range(1, N_EXPERTS // EXPERTS_PER_GROUP))
    pid = tcls - n_pairs * grp
    lo = sum(jnp.where(pid == float(i), float(a), 0.0) for i, (a, _) in enumerate(_PAIRS))
    hi = sum(jnp.where(pid == float(i), float(b), 0.0) for i, (_, b) in enumerate(_PAIRS))
    used = jnp.broadcast_to(end[N_CLASSES - 1:N_CLASSES] * (1.0 / EXPERT_TILE), (1, 128))
    rows = [grp * EXPERTS_PER_GROUP + lo, grp * EXPERTS_PER_GROUP + hi, used, jnp.zeros((5, 128), F32)]
    meta_ref[...] = jnp.concatenate(rows, axis=0).astype(jnp.int32)


def _plan(cls):
    n_chunks, width = cls.shape
    tri = jnp.asarray(np.triu(np.ones((width, width), np.float32), k=1)).astype(BF16)
    return pl.pallas_call(
        functools.partial(_plan_kernel, n_chunks=n_chunks),
        out_shape=[jax.ShapeDtypeStruct((n_chunks, width), jnp.int32),
                   jax.ShapeDtypeStruct((8, 128), jnp.int32)],
        name="moe_plan",
    )(cls, tri)


def _row_copy(src_ref, src_row, dst_ref, dst_row, sem):
    return pltpu.make_async_copy(src_ref.at[pl.ds(src_row, 1)], dst_ref.at[pl.ds(dst_row, 1)], sem)


def _dispatch_kernel(pos_ref, src_ref, init_ref, dst_ref, sem):
    del init_ref
    n_sub, width = pos_ref.shape
    for q in range(n_sub):
        def issue(r, carry, q=q):
            _row_copy(src_ref, q * width + r, dst_ref, pos_ref[q, r], sem).start()
            return carry
        lax.fori_loop(0, width, issue, 0, unroll=8)

    def drain(r, carry):
        _row_copy(src_ref, 0, dst_ref, 0, sem).wait()
        return carry
    lax.fori_loop(0, n_sub * width, drain, 0, unroll=8)


def _dispatch(pos, rows, n_sorted):
    n_steps, n_sub, width = pos.shape
    t, w = rows.shape
    assert t == n_steps * n_sub * width
    return pl.pallas_call(
        _dispatch_kernel,
        grid=(n_steps,),
        in_specs=[pl.BlockSpec((None, n_sub, width), lambda i: (i, 0, 0), memory_space=pltpu.SMEM),
                  pl.BlockSpec((n_sub * width, w), lambda i: (i, 0)),
                  pl.BlockSpec(memory_space=pl.ANY)],
        out_specs=pl.BlockSpec(memory_space=pl.ANY),
        out_shape=jax.ShapeDtypeStruct((n_sorted, w), F32),
        scratch_shapes=[pltpu.SemaphoreType.DMA(())],
        input_output_aliases={2: 0},
        compiler_params=_params(("arbitrary",)),
        name="moe_dispatch",
    )(pos, rows, jnp.zeros((n_sorted, w), F32))


def _expert_pair_kernel(lo_ref, hi_ref, used_ref, xs_ref, gu_lo_ref, dn_lo_ref, gu_hi_ref, dn_hi_ref, o_ref):
    del lo_ref, hi_ref
    i = pl.program_id(0)

    @pl.when(i < used_ref[0])
    def _():
        x = xs_ref[:, :D_MODEL].astype(BF16)

        def ffn(gu_ref, dn_ref):
            ab = _dot(x, gu_ref[...])
            a = ab[:, :EXPERT_FF]
            return _dot((a * _sigmoid(a) * ab[:, EXPERT_FF:]).astype(BF16), dn_ref[...])

        o_ref[...] = (xs_ref[:, D_MODEL:D_MODEL + 1] * ffn(gu_lo_ref, dn_lo_ref)
                      + xs_ref[:, D_MODEL + 1:D_MODEL + 2] * ffn(gu_hi_ref, dn_hi_ref))

    @pl.when(i >= used_ref[0])
    def _():
        o_ref[...] = jnp.zeros_like(o_ref)


def _expert_pairs(meta, xs, w_gu, w_down, layer):
    n_sorted, w = xs.shape
    d = D_MODEL
    te = EXPERT_TILE
    gu = lambda ref: pl.BlockSpec((None, None, d, 2 * EXPERT_FF),
                                  lambda i, lo, hi, used: (layer, ref(lo, hi)[i], 0, 0))
    dn = lambda ref: pl.BlockSpec((None, None, EXPERT_FF, d),
                                  lambda i, lo, hi, used: (layer, ref(lo, hi)[i], 0, 0))
    first, second = (lambda lo, hi: lo), (lambda lo, hi: hi)
    grid_spec = pltpu.PrefetchScalarGridSpec(
        num_scalar_prefetch=3,
        grid=(n_sorted // te,),
        in_specs=[pl.BlockSpec((te, w), lambda i, lo, hi, used: (i, 0)),
                  gu(first), dn(first), gu(second), dn(second)],
        out_specs=pl.BlockSpec((te, d), lambda i, lo, hi, used: (i, 0)))
    return pl.pallas_call(
        _expert_pair_kernel,
        grid_spec=grid_spec,
        out_shape=jax.ShapeDtypeStruct((n_sorted, d), F32),
        compiler_params=_params(("arbitrary",)),
        name="moe_expert_pairs",
    )(meta[0], meta[1], meta[2, :1], xs, w_gu, w_down, w_gu, w_down)


def _combine_kernel(pos_ref, x_ref, g2_ref, *rest, final):
    if final:
        g_ref, ys_ref, o_ref, z_ref, sem = rest
    else:
        ys_ref, o_ref, z_ref, sem = rest
    n_sub, width = pos_ref.shape
    for q in range(n_sub):
        def issue(r, carry, q=q):
            _row_copy(ys_ref, pos_ref[q, r], z_ref, q * width + r, sem).start()
            return carry
        lax.fori_loop(0, width, issue, 0, unroll=8)

    def drain(r, carry):
        _row_copy(ys_ref, 0, z_ref, 0, sem).wait()
        return carry
    lax.fori_loop(0, n_sub * width, drain, 0, unroll=8)

    for q in range(n_sub):
        rows = slice(q * width, (q + 1) * width)
        x = x_ref[rows, :] + g2_ref[q] * z_ref[rows, :]
        if final:
            ms = jnp.mean(x * x, axis=-1, keepdims=True)
            x = x * lax.rsqrt(ms + EPS) * g_ref[...]
        o_ref[rows, :] = x


def _combine(pos, x, ys, g2, final_g=None):
    n_steps, n_sub, width = pos.shape
    t, d = x.shape
    tm = n_sub * width
    row = pl.BlockSpec((tm, d), lambda i: (i, 0))
    in_specs = [pl.BlockSpec((None, n_sub, width), lambda i: (i, 0, 0), memory_space=pltpu.SMEM),
                row,
                pl.BlockSpec((n_sub, 1, d), lambda i: (i, 0, 0))]
    args = [pos, x, g2]
    if final_g is not None:
        in_specs.append(_const_spec((1, d)))
        args.append(final_g)
    in_specs.append(pl.BlockSpec(memory_space=pl.ANY))
    args.append(ys)
    return pl.pallas_call(
        functools.partial(_combine_kernel, final=final_g is not None),
        grid=(n_steps,),
        in_specs=in_specs,
        out_specs=row,
        out_shape=jax.ShapeDtypeStruct((t, d), F32),
        scratch_shapes=[pltpu.VMEM((tm, d), F32), pltpu.SemaphoreType.DMA(())],
        compiler_params=_params(("arbitrary",)),
        name="moe_combine_final_norm" if final_g is not None else "moe_combine",
    )(*args)


def _moe(rows, cls, xa, modtab, w_gu, w_down, layer, n_lat, final_g=None):
    b, n, w = rows.shape
    d = xa.shape[-1]
    t = b * n
    n_tiles = n // ROW_TILE
    n_chunks = b * n_tiles
    n_sub = DISPATCH_CHUNKS if n_chunks % DISPATCH_CHUNKS == 0 else 1
    n_sorted = t + N_CLASSES * EXPERT_TILE
    assert n_sorted // EXPERT_TILE <= 128
    pos, meta = _plan(cls[:, :, 0, :].reshape(n_chunks, ROW_TILE))
    pos = pos.reshape(n_chunks // n_sub, n_sub, ROW_TILE)
    xs = _dispatch(pos, rows.reshape(t, w), n_sorted)
    ys = _expert_pairs(meta, xs, w_gu, w_down, layer)
    g2 = modtab[:, :, 5, :]
    g2 = jnp.concatenate([jnp.broadcast_to(g2[:, :1], (b, n_lat, d)), g2[:, 1:]], axis=1)[:, :n_tiles]
    out = _combine(pos, xa.reshape(t, d), ys, g2.reshape(n_chunks, 1, d), final_g)
    return out.reshape(b, n, d)


def _rope_tables(seq, ctx):
    t = jnp.arange(seq, dtype=jnp.int32)
    row = (t // GRID_W).astype(F32)
    col = (t % GRID_W).astype(F32)
    n_pairs = HEAD_DIM // 4
    inv_freq = ROPE_THETA ** (-jnp.arange(n_pairs, dtype=F32) / n_pairs)
    ang = jnp.concatenate([row[:, None] * inv_freq, col[:, None] * inv_freq], axis=-1)
    cos = jnp.repeat(jnp.cos(ang), 2, axis=-1)
    sin = jnp.repeat(jnp.sin(ang), 2, axis=-1) * jnp.tile(jnp.array([-1.0, 1.0], F32), HEAD_DIM // 2)
    cos = jnp.concatenate([cos, jnp.ones((ctx, HEAD_DIM), F32)], axis=0)
    sin = jnp.concatenate([sin, jnp.zeros((ctx, HEAD_DIM), F32)], axis=0)
    ck, sk = jnp.tile(cos, (1, GQA_KV_HEADS)), jnp.tile(sin, (1, GQA_KV_HEADS))
    cq, sq = jnp.tile(cos, (1, N_HEADS)) * Q_SCALE, jnp.tile(sin, (1, N_HEADS)) * Q_SCALE
    return cq, sq, ck, sk


def _group_sum_matrix():
    idx = np.arange(MIX_WIDTH) // HEAD_DIM
    return jnp.asarray(idx[:, None] == idx[None, :], dtype=BF16)


def _channel_dft_matrix():
    c = np.arange(FOURIER_GROUP_DIM)
    ang = 2.0 * np.pi * ((c[:, None] * c[None, :]) % FOURIER_GROUP_DIM) / FOURIER_GROUP_DIM
    n_groups = MIX_WIDTH // FOURIER_GROUP_DIM
    eye = np.eye(n_groups)
    m = np.concatenate([np.kron(eye, np.cos(ang)), np.kron(eye, np.sin(ang))], axis=1)
    return jnp.asarray(m, dtype=F32).astype(BF16)


def kernel(x, c, ctx, c_ctx, w_mod, b_mod, norm1_g, norm2_g, w_in, q_norm_g, k_norm_g, na_rpb, conv_w, conv_b,
           conv_norm_g, w_branch, w_out, w_router, router_bias, w_expert_gu, w_expert_down, final_norm_g):
    b, seq, d = x.shape
    n_ctx = ctx.shape[1]
    depth = w_mod.shape[0]
    assert d == D_MODEL and seq % ROW_TILE == 0 and n_ctx == ROW_TILE and seq % n_ctx == 0
    assert seq // GRID_W >= NA_BAND_ROWS and b <= 7
    n_lat = seq // ROW_TILE
    n_all = n_lat + 1

    c_all = jnp.zeros((8, d), F32).at[:b].set(c).at[b].set(c_ctx)
    m = _modulation(c_all, w_mod, b_mod)
    m_lat = m[:, :b].reshape(depth, b, 1, 6, d)
    m_ctx = jnp.broadcast_to(m[:, b].reshape(depth, 1, 1, 6, d), (depth, b, 1, 6, d))
    modtab = jnp.concatenate([m_lat, m_ctx], axis=2)

    tabs = _rope_tables(seq, n_ctx)
    gsum = _group_sum_matrix()
    dcs = _channel_dft_matrix()
    dft_lat = _dft_mats(seq)
    dft_ctx = _dft_mats(n_ctx)
    w_router_t = w_router.T
    rb = router_bias.reshape(N_EXPERTS, 1)

    w_in_b, w_branch_b, w_out_b = w_in.astype(BF16), w_branch.astype(BF16), w_out.astype(BF16)
    w_gu_b, w_down_b = w_expert_gu.astype(BF16), w_expert_down.astype(BF16)
    na_bias = _na_bias(na_rpb, seq)

    stream = (x, ctx, 0)
    xa = None
    for l in range(depth):
        last = l == depth - 1
        n_tiles = n_lat if last else n_all
        qg = jnp.tile(q_norm_g[l], N_HEADS).reshape(1, -1)
        kg = jnp.tile(k_norm_g[l], GQA_KV_HEADS).reshape(1, -1)
        q, k, v, nq, nk, nv, fab, cu, gl = _inproj(
            stream, modtab[l], norm1_g[l].reshape(1, d), w_in_b, l, tabs, (qg, kg, gsum, dcs), n_lat)
        y_gqa = _gqa(q, k, v, seq, n_ctx, n_tiles)
        y_na = _na(nq, nk, nv, na_bias, l, seq, n_ctx, n_tiles)
        y_fn = _fourier_latent(fab, seq, dft_lat)
        y_fn_ctx = y_fn if last else _fourier_ctx(fab, seq, n_ctx, dft_ctx)
        y_cv = _conv(cu, conv_w[l], conv_b[l].reshape(1, -1), conv_norm_g[l].reshape(1, -1), seq, n_tiles)
        xa, rows, cls = _merge((y_gqa, y_na, y_fn, y_fn_ctx, y_cv), gl, stream, modtab[l], norm2_g[l].reshape(1, d),
                               w_branch_b, w_out_b, l, w_router_t, rb, n_lat, n_tiles)
        xa = _moe(rows, cls, xa, modtab[l], w_gu_b, w_down_b, l, n_lat,
                  final_norm_g.reshape(1, d) if last else None)
        stream = (xa, xa, n_lat)
    return xa
```

```python
import functools

import numpy as np
import jax
import jax.numpy as jnp
from jax import lax
from jax.experimental import pallas as pl
from jax.experimental.pallas import tpu as pltpu

D_MODEL = 1024
GRID_W = 64
MIX_WIDTH = 512
HEAD_DIM = 64
N_HEADS = 8
GQA_KV_HEADS = 2
NA_WIN_ROWS = 8
NA_WIN_COLS = 16
FOURIER_GROUP_DIM = 128
CONV_KERNEL = 31
N_BRANCHES = 4
N_EXPERTS = 16
EXPERTS_PER_GROUP = 4
EXPERT_FF = 512
ROPE_THETA = 10000.0
EPS = 1e-6
ATTN_SCALE = HEAD_DIM ** -0.5
LOG2E = float(np.log2(np.e))
Q_SCALE = ATTN_SCALE * LOG2E

_OFF_GQ, _OFF_GK, _OFF_GV = 0, 512, 640
_OFF_NQ, _OFF_NK, _OFF_NV = 768, 1280, 1792
_OFF_FU, _OFF_CU, _OFF_GL = 2304, 2816, 3840
IN_WIDTH = 7936

ROW_TILE = 256
NA_BLOCK_ROWS = ROW_TILE // GRID_W
NA_BAND_ROWS = NA_BLOCK_ROWS + NA_WIN_ROWS
ROW_EXTRA = 128
EXPERT_TILE = 256
DISPATCH_CHUNKS = 4
CONV_HALO = 16
MASK_VALUE = -1e30
VMEM_LIMIT = 56 * 1024 * 1024

F32 = jnp.float32
BF16 = jnp.bfloat16


def _params(sem, vmem=VMEM_LIMIT):
    return pltpu.CompilerParams(dimension_semantics=sem, vmem_limit_bytes=vmem)


def _const_spec(shape):
    nd = len(shape)
    return pl.BlockSpec(shape, lambda *_: (0,) * nd, pipeline_mode=pl.Buffered(1))


def _layer_spec(shape, layer):
    nd = len(shape)
    return pl.BlockSpec((None,) + tuple(shape), lambda *_: (layer,) + (0,) * nd, pipeline_mode=pl.Buffered(1))


def _stream_specs(tm, width, n_lat, ctx_block):
    return [pl.BlockSpec((None, tm, width), lambda bb, i: (bb, jnp.minimum(i, n_lat - 1), 0)),
            pl.BlockSpec((None, tm, width), lambda bb, i: (bb, ctx_block, 0))]


def _stream_tile(lat_ref, ctx_ref, n_lat):
    return jnp.where(pl.program_id(1) < n_lat, lat_ref[...], ctx_ref[...])


def _sigmoid(v):
    return 0.5 * jnp.tanh(0.5 * v) + 0.5


def _sigmoid_exp(v):
    return 1.0 / (1.0 + jnp.exp(-v))


def _dot(a, b):
    return jnp.dot(a, b, preferred_element_type=F32)


def _dot_t(a, b):
    return lax.dot_general(a, b, (((1,), (1,)), ((), ())), preferred_element_type=F32)


def _mod_kernel(c_ref, w_ref, b_ref, o_ref):
    cc = c_ref[...]
    a = cc * _sigmoid(cc)
    o_ref[...] = jnp.dot(a, w_ref[...], preferred_element_type=F32,
                         precision=lax.Precision.HIGHEST) + b_ref[...]


def _modulation(c_all, w_mod, b_mod):
    depth, d, n = w_mod.shape
    tn = 1536
    return pl.pallas_call(
        _mod_kernel,
        grid=(depth, n // tn),
        in_specs=[pl.BlockSpec((8, d), lambda l, j: (0, 0)),
                  pl.BlockSpec((None, d, tn), lambda l, j: (l, 0, j)),
                  pl.BlockSpec((None, 1, tn), lambda l, j: (l, 0, j))],
        out_specs=pl.BlockSpec((None, 8, tn), lambda l, j: (l, 0, j)),
        out_shape=jax.ShapeDtypeStruct((depth, 8, n), F32),
        compiler_params=_params(("arbitrary", "arbitrary")),
        name="modulation",
    )(c_all, w_mod, b_mod.reshape(depth, 1, n))


def _modulated_norm(x, g, shift, scale):
    ms = jnp.mean(x * x, axis=-1, keepdims=True)
    return x * lax.rsqrt(ms + EPS) * g * (1.0 + scale) + shift


def _head_norm(q, gsum, gain):
    sq = q * q
    hi = sq.astype(BF16)
    lo = (sq - hi.astype(F32)).astype(BF16)
    ss = (_dot(hi, gsum) + _dot(lo, gsum)) * (1.0 / HEAD_DIM)
    return q * lax.rsqrt(ss + EPS) * gain


def _rope(q, cos, sin_signed):
    n = q.shape[-1]
    lane = lax.broadcasted_iota(jnp.int32, q.shape, 1)
    swapped = jnp.where(lane % 2 == 0, pltpu.roll(q, n - 1, 1), pltpu.roll(q, 1, 1))
    return q * cos + swapped * sin_signed


def _inproj_kernel(xl_ref, xc_ref, mod_ref, g_ref, w_ref, cq_ref, sq_ref, ck_ref, sk_ref, qg_ref, kg_ref,
                   gsum_ref, dcs_ref,
                   q_ref, k_ref, v_ref, nq_ref, nk_ref, nv_ref, fab_ref, cu_ref, gl_ref, *, n_lat):
    mod = mod_ref[...]
    h = _modulated_norm(_stream_tile(xl_ref, xc_ref, n_lat), g_ref[...], mod[0:1], mod[1:2]).astype(BF16)

    def seg(off, width):
        return _dot(h, w_ref[:, off:off + width])

    gsum = gsum_ref[...]
    q = _head_norm(seg(_OFF_GQ, 512), gsum, qg_ref[...])
    q_ref[...] = _rope(q, cq_ref[...], sq_ref[...]).astype(BF16)
    kv = seg(_OFF_GK, 256)
    k = _head_norm(kv[:, :128], gsum[:128, :128], kg_ref[...])
    k_ref[...] = _rope(k, ck_ref[...], sk_ref[...]).astype(BF16)
    v_ref[...] = kv[:, 128:].astype(BF16)
    nq_ref[...] = (seg(_OFF_NQ, 512) * Q_SCALE).astype(BF16)
    nk_ref[...] = seg(_OFF_NK, 512).astype(BF16)
    nv_ref[...] = seg(_OFF_NV, 512).astype(BF16)
    ab = _dot(seg(_OFF_FU, 512).astype(BF16), dcs_ref[...])
    fab_ref[0] = ab[:, :512].astype(BF16)
    fab_ref[1] = ab[:, 512:].astype(BF16)
    for j in range(2):
        cu_ref[:, j * 512:(j + 1) * 512] = seg(_OFF_CU + j * 512, 512).astype(BF16)
    for j in range(4):
        gl_ref[:, j * 1024:(j + 1) * 1024] = (seg(_OFF_GL + j * 1024, 1024) * 0.5).astype(BF16)


def _inproj(stream, modtab, g1, w_in, layer, tabs, consts, n_lat):
    x_lat, x_ctx, ctx_block = stream
    b, _, d = x_lat.shape
    tm = ROW_TILE
    nt = (n_lat + 1) * tm
    cq, sq, ck, sk = tabs
    qg, kg, gsum, dcs = consts
    row = lambda width: pl.BlockSpec((None, tm, width), lambda bb, i: (bb, i, 0))
    tab = lambda width: pl.BlockSpec((tm, width), lambda bb, i: (i, 0))
    out_shapes = [jax.ShapeDtypeStruct((b, nt, wd), BF16) for wd in (512, 128, 128, 512, 512, 512)]
    out_shapes += [jax.ShapeDtypeStruct((b, 2, nt, 512), BF16),
                   jax.ShapeDtypeStruct((b, nt, 1024), BF16),
                   jax.ShapeDtypeStruct((b, nt, 4096), BF16)]
    out_specs = [row(512), row(128), row(128), row(512), row(512), row(512),
                 pl.BlockSpec((None, 2, tm, 512), lambda bb, i: (bb, 0, i, 0)),
                 row(1024), row(4096)]
    return pl.pallas_call(
        functools.partial(_inproj_kernel, n_lat=n_lat),
        grid=(b, nt // tm),
        in_specs=_stream_specs(tm, d, n_lat, ctx_block) + [
                  pl.BlockSpec((None, None, 6, d), lambda bb, i: (bb, i // n_lat, 0, 0)),
                  _const_spec((1, d)),
                  _layer_spec((d, IN_WIDTH), layer),
                  tab(512), tab(512), tab(128), tab(128),
                  _const_spec((1, 512)), _const_spec((1, 128)),
                  _const_spec((512, 512)), _const_spec((512, 1024))],
        out_specs=out_specs,
        out_shape=out_shapes,
        compiler_params=_params(("parallel", "parallel")),
        name="inproj",
    )(x_lat, x_ctx, modtab, g1, w_in, cq, sq, ck, sk, qg, kg, gsum, dcs)


def _with_ones(v):
    return jnp.concatenate([v, jnp.ones_like(v)], axis=1)


def _softmax_pv(parts):
    m = None
    for s, _ in parts:
        mi = jnp.max(s, axis=-1, keepdims=True)
        m = mi if m is None else jnp.maximum(m, mi)
    acc = None
    for s, v in parts:
        oi = _dot(jnp.exp2(s - m).astype(BF16), v)
        acc = oi if acc is None else acc + oi
    return acc[:, :HEAD_DIM] / acc[:, HEAD_DIM:HEAD_DIM + 1]


def _gqa_kernel(q_ref, k_ref, v_ref, o_ref, *, seq, ctx, n_lat):
    i = pl.program_id(1)
    group = N_HEADS // GQA_KV_HEADS

    def attend(lo, n):
        for kv in range(GQA_KV_HEADS):
            c0 = kv * HEAD_DIM
            kk = k_ref[lo:lo + n, c0:c0 + HEAD_DIM]
            vv = _with_ones(v_ref[lo:lo + n, c0:c0 + HEAD_DIM])
            for h in range(kv * group, (kv + 1) * group):
                qh = q_ref[:, h * HEAD_DIM:(h + 1) * HEAD_DIM]
                o = _softmax_pv([(_dot_t(qh, kk), vv)])
                o_ref[:, h * HEAD_DIM:(h + 1) * HEAD_DIM] = o.astype(o_ref.dtype)

    @pl.when(i < n_lat)
    def _():
        attend(0, seq + ctx)

    @pl.when(i >= n_lat)
    def _():
        attend(seq, ctx)


def _gqa(q, k, v, seq, ctx, n_tiles):
    b, nt, _ = q.shape
    tm = ROW_TILE
    n_lat = seq // tm
    return pl.pallas_call(
        functools.partial(_gqa_kernel, seq=seq, ctx=ctx, n_lat=n_lat),
        grid=(b, n_tiles),
        in_specs=[pl.BlockSpec((None, tm, 512), lambda bb, i: (bb, i, 0)),
                  pl.BlockSpec((None, nt, 128), lambda bb, i: (bb, 0, 0)),
                  pl.BlockSpec((None, nt, 128), lambda bb, i: (bb, 0, 0))],
        out_specs=pl.BlockSpec((None, tm, 512), lambda bb, i: (bb, i, 0)),
        out_shape=jax.ShapeDtypeStruct((b, n_tiles * tm, 512), BF16),
        compiler_params=_params(("parallel", "parallel")),
        name="gqa_attention",
    )(q, k, v)


def _na_kernel(q_ref, k_ref, v_ref, bias_ref, o_ref, *, seq, ctx, n_lat):
    i = pl.program_id(1)
    band = NA_BAND_ROWS * GRID_W
    rows = seq // GRID_W

    @pl.when(i < n_lat)
    def _():
        band_row = jnp.clip(i * NA_BLOCK_ROWS - NA_WIN_ROWS // 2, 0, rows - NA_BAND_ROWS)
        start = pl.multiple_of(band_row * GRID_W, ROW_TILE)
        for h in range(N_HEADS):
            hs = slice(h * HEAD_DIM, (h + 1) * HEAD_DIM)
            qh = q_ref[:, hs]
            s_band = _dot_t(qh, k_ref[pl.ds(start, band), hs]) + bias_ref[h]
            s_ctx = _dot_t(qh, k_ref[seq:seq + ctx, hs])
            o = _softmax_pv([(s_band, _with_ones(v_ref[pl.ds(start, band), hs])),
                             (s_ctx, _with_ones(v_ref[seq:seq + ctx, hs]))])
            o_ref[:, hs] = o.astype(o_ref.dtype)

    @pl.when(i >= n_lat)
    def _():
        for h in range(N_HEADS):
            hs = slice(h * HEAD_DIM, (h + 1) * HEAD_DIM)
            o = _softmax_pv([(_dot_t(q_ref[:, hs], k_ref[seq:seq + ctx, hs]), _with_ones(v_ref[seq:seq + ctx, hs]))])
            o_ref[:, hs] = o.astype(o_ref.dtype)


def _na(q, k, v, bias, layer, seq, ctx, n_tiles):
    b, nt, _ = q.shape
    tm = ROW_TILE
    n_lat = seq // tm
    band = NA_BAND_ROWS * GRID_W

    def bias_idx(bb, i):
        return (jnp.where(i == 0, 0, jnp.where(i >= n_lat - 1, 2, 1)), layer, 0, 0, 0)

    return pl.pallas_call(
        functools.partial(_na_kernel, seq=seq, ctx=ctx, n_lat=n_lat),
        grid=(b, n_tiles),
        in_specs=[pl.BlockSpec((None, tm, 512), lambda bb, i: (bb, i, 0)),
                  pl.BlockSpec((None, nt, 512), lambda bb, i: (bb, 0, 0)),
                  pl.BlockSpec((None, nt, 512), lambda bb, i: (bb, 0, 0)),
                  pl.BlockSpec((None, None, N_HEADS, tm, band), bias_idx)],
        out_specs=pl.BlockSpec((None, tm, 512), lambda bb, i: (bb, i, 0)),
        out_shape=jax.ShapeDtypeStruct((b, n_tiles * tm, 512), BF16),
        compiler_params=_params(("parallel", "parallel")),
        name="neighbourhood_attention",
    )(q, k, v, bias)


def _bias_rows_kernel(rpb_ref, onehot_ref, mask_ref, o_ref):
    o_ref[...] = jnp.dot(rpb_ref[...], onehot_ref[...], preferred_element_type=F32,
                         precision=lax.Precision.HIGHEST) + mask_ref[...]


def _na_bias(rpb, seq):
    n_dr, n_dc = 2 * NA_WIN_ROWS - 1, 2 * NA_WIN_COLS - 1
    qc = np.arange(GRID_W)
    c0 = np.clip(qc - NA_WIN_COLS // 2, 0, GRID_W - NA_WIN_COLS)
    col_ok = (qc[None, :] >= c0[:, None]) & (qc[None, :] < c0[:, None] + NA_WIN_COLS)
    col_idx = qc[None, :] - qc[:, None] + NA_WIN_COLS - 1
    onehot = np.zeros((128, GRID_W * GRID_W), np.float32)
    for j in range(n_dc):
        onehot[j] = ((col_idx == j) & col_ok).reshape(-1)
    mask = np.where(col_ok, 0.0, MASK_VALUE).astype(np.float32).reshape(1, -1)
    n_heads = rpb.shape[0] * rpb.shape[1]
    n_rows = -(-n_heads * n_dr // 128) * 128
    rpb_rows = jnp.zeros((n_rows, 128), F32).at[:n_heads * n_dr, :n_dc].set(rpb.astype(F32).reshape(-1, n_dc))
    blocks = pl.pallas_call(
        _bias_rows_kernel,
        out_shape=jax.ShapeDtypeStruct((n_rows, GRID_W * GRID_W), F32),
        name="na_bias_rows",
    )(rpb_rows, jnp.asarray(onehot), jnp.asarray(mask))
    blocks = (blocks[:n_heads * n_dr] * LOG2E).reshape(n_heads, n_dr, GRID_W, GRID_W)
    masked = jnp.full((n_heads, GRID_W, GRID_W), MASK_VALUE, F32)

    rows = seq // GRID_W
    n_blk = rows // NA_BLOCK_ROWS
    tables = []
    for blk in (0, 1, n_blk - 1):
        band_row = int(np.clip(blk * NA_BLOCK_ROWS - NA_WIN_ROWS // 2, 0, rows - NA_BAND_ROWS))
        strips = []
        for qr in range(blk * NA_BLOCK_ROWS, (blk + 1) * NA_BLOCK_ROWS):
            r0 = int(np.clip(qr - NA_WIN_ROWS // 2, 0, rows - NA_WIN_ROWS))
            strip = [blocks[:, kr - qr + NA_WIN_ROWS - 1] if r0 <= kr < r0 + NA_WIN_ROWS else masked
                     for kr in range(band_row, band_row + NA_BAND_ROWS)]
            strips.append(jnp.concatenate(strip, axis=-1))
        tables.append(jnp.concatenate(strips, axis=1))
    return jnp.stack(tables).reshape(3, rpb.shape[0], rpb.shape[1], ROW_TILE, NA_BAND_ROWS * GRID_W)


def _dft_kernel(wc_ref, ws_ref, a_ref, b_ref, o_ref):
    y = _dot(wc_ref[...], a_ref[...]) - _dot(ws_ref[...], b_ref[...])
    o_ref[...] = y.astype(o_ref.dtype)


def _dft_mats(n):
    r = int(round(np.sqrt(n)))
    assert r * r == n
    k = jnp.arange(n, dtype=jnp.int32)[None, :]
    j = jnp.arange(r, dtype=jnp.int32)[:, None]
    a = ((k * j) % r).astype(F32) * (2.0 * np.pi / r)
    bb = ((k * j) % n).astype(F32) * (2.0 * np.pi / n)
    ca, sa = jnp.cos(a)[:, None, :], jnp.sin(a)[:, None, :]
    cb, sb = jnp.cos(bb)[None, :, :], jnp.sin(bb)[None, :, :]
    scale = 1.0 / np.sqrt(n * FOURIER_GROUP_DIM)
    cos = ((ca * cb - sa * sb) * scale).reshape(n, n)
    sin = ((sa * cb + ca * sb) * scale).reshape(n, n)
    return cos.astype(BF16), sin.astype(BF16)


def _fourier_latent(fab, seq, mats):
    b, _, nt, w = fab.shape
    tf = 512 if seq % 512 == 0 else ROW_TILE
    wc, ws = mats
    return pl.pallas_call(
        _dft_kernel,
        grid=(seq // tf, b),
        in_specs=[pl.BlockSpec((tf, seq), lambda i, bb: (i, 0)),
                  pl.BlockSpec((tf, seq), lambda i, bb: (i, 0)),
                  pl.BlockSpec((None, None, seq, w), lambda i, bb: (bb, 0, 0, 0)),
                  pl.BlockSpec((None, None, seq, w), lambda i, bb: (bb, 1, 0, 0))],
        out_specs=pl.BlockSpec((None, tf, w), lambda i, bb: (bb, i, 0)),
        out_shape=jax.ShapeDtypeStruct((b, seq, w), BF16),
        compiler_params=_params(("parallel", "parallel")),
        name="fourier_latent",
    )(wc, ws, fab, fab)


def _fourier_ctx(fab, seq, ctx, mats):
    b, _, nt, w = fab.shape
    blk = seq // ctx
    wc, ws = mats
    return pl.pallas_call(
        _dft_kernel,
        grid=(b,),
        in_specs=[_const_spec((ctx, ctx)), _const_spec((ctx, ctx)),
                  pl.BlockSpec((None, None, ctx, w), lambda bb: (bb, 0, blk, 0)),
                  pl.BlockSpec((None, None, ctx, w), lambda bb: (bb, 1, blk, 0))],
        out_specs=pl.BlockSpec((None, ctx, w), lambda bb: (bb, 0, 0)),
        out_shape=jax.ShapeDtypeStruct((b, ctx, w), BF16),
        compiler_params=_params(("parallel",)),
        name="fourier_ctx",
    )(wc, ws, fab, fab)


def _conv_kernel(cu_ref, w_ref, b_ref, g_ref, o_ref, z_ref, *, n_lat, n_tiles):
    tm = ROW_TILE
    halo = CONV_HALO
    zeros = jnp.zeros((halo, MIX_WIDTH), F32)
    lat_end = halo + n_lat * tm
    z_ref[0:halo, :] = zeros
    z_ref[lat_end:lat_end + halo, :] = zeros
    ctx_end = lat_end + halo + (n_tiles - n_lat) * tm
    z_ref[ctx_end:ctx_end + halo, :] = zeros

    def tile_base(t):
        return pl.multiple_of(t * tm + halo + jnp.where(t >= n_lat, halo, 0), 8)

    def glu(t, carry):
        r = pl.multiple_of(t * tm, tm)
        a = cu_ref[pl.ds(r, tm), 0:MIX_WIDTH].astype(F32)
        g = cu_ref[pl.ds(r, tm), MIX_WIDTH:2 * MIX_WIDTH].astype(F32)
        z_ref[pl.ds(tile_base(t), tm), :] = a * _sigmoid(g)
        return carry

    lax.fori_loop(0, n_tiles, glu, 0)

    def conv(t, carry):
        start = pl.multiple_of(tile_base(t) - halo, 8)
        cols = []
        for c in range(MIX_WIDTH // 128):
            cs = slice(c * 128, (c + 1) * 128)
            win = z_ref[pl.ds(start, tm + 2 * halo), cs]
            acc = jnp.zeros((tm, 128), F32)
            n_win = tm + 2 * halo
            for s in range(8):
                shifted = win if s == 0 else pltpu.roll(win, n_win - s, 0)
                for j in range(CONV_KERNEL):
                    off = halo - CONV_KERNEL // 2 + j
                    if off % 8 == s:
                        acc = acc + shifted[off - s:off - s + tm, :] * w_ref[j:j + 1, cs]
            cols.append(acc)
        y = jnp.concatenate(cols, axis=1) + b_ref[...]
        ms = jnp.mean(y * y, axis=-1, keepdims=True)
        y = y * lax.rsqrt(ms + EPS) * g_ref[...]
        r = pl.multiple_of(t * tm, tm)
        o_ref[pl.ds(r, tm), :] = (y * _sigmoid(y)).astype(o_ref.dtype)
        return carry

    lax.fori_loop(0, n_tiles, conv, 0)


def _conv(cu, conv_w, conv_b, conv_g, seq, n_tiles):
    b, nt, _ = cu.shape
    n_lat = seq // ROW_TILE
    return pl.pallas_call(
        functools.partial(_conv_kernel, n_lat=n_lat, n_tiles=n_tiles),
        grid=(b,),
        in_specs=[pl.BlockSpec((None, nt, 2 * MIX_WIDTH), lambda bb: (bb, 0, 0)),
                  _const_spec((CONV_KERNEL, MIX_WIDTH)),
                  _const_spec((1, MIX_WIDTH)), _const_spec((1, MIX_WIDTH))],
        out_specs=pl.BlockSpec((None, n_tiles * ROW_TILE, MIX_WIDTH), lambda bb: (bb, 0, 0)),
        out_shape=jax.ShapeDtypeStruct((b, n_tiles * ROW_TILE, MIX_WIDTH), BF16),
        scratch_shapes=[pltpu.VMEM((nt + 3 * CONV_HALO, MIX_WIDTH), F32)],
        compiler_params=_params(("parallel",)),
        name="conformer_conv",
    )(cu, conv_w, conv_b, conv_g)


_PAIRS = tuple((a, b) for a in range(EXPERTS_PER_GROUP) for b in range(a + 1, EXPERTS_PER_GROUP))
N_CLASSES = (N_EXPERTS // EXPERTS_PER_GROUP) * len(_PAIRS)


def _route(logits_t, bias):
    score = _sigmoid_exp(logits_t)
    sel = score + bias
    sel_r = [sel[e:e + 1] for e in range(N_EXPERTS)]
    n_groups = N_EXPERTS // EXPERTS_PER_GROUP

    def beats(a, ia, b, ib):
        return (a >= b) if ia < ib else (a > b)

    picked = []
    for g in range(n_groups):
        ids = range(g * EXPERTS_PER_GROUP, (g + 1) * EXPERTS_PER_GROUP)
        for e in ids:
            rank = sum(beats(sel_r[o], o, sel_r[e], e).astype(F32) for o in ids if o != e)
            picked.append(rank < 2.0)
    group_score = []
    for g in range(n_groups):
        ids = range(g * EXPERTS_PER_GROUP, (g + 1) * EXPERTS_PER_GROUP)
        group_score.append(sum(jnp.where(picked[e], sel_r[e], 0.0) for e in ids))
    cls = s_lo = s_hi = 0.0
    for g in range(n_groups):
        rank = sum(beats(group_score[o], o, group_score[g], g).astype(F32) for o in range(n_groups) if o != g)
        best = rank < 1.0
        for pid, (a, b) in enumerate(_PAIRS):
            ea, eb = g * EXPERTS_PER_GROUP + a, g * EXPERTS_PER_GROUP + b
            both = picked[ea] & picked[eb] & best
            cls = cls + jnp.where(both, float(g * len(_PAIRS) + pid), 0.0)
            s_lo = s_lo + jnp.where(both, score[ea:ea + 1], 0.0)
            s_hi = s_hi + jnp.where(both, score[eb:eb + 1], 0.0)
    denom = s_lo + s_hi
    return cls, s_lo / denom, s_hi / denom


def _merge_kernel(yg_ref, yn_ref, yf_ref, yfc_ref, yc_ref, gl_ref, xl_ref, xc_ref, mod_ref, g2_ref, wb_ref, wo_ref,
                  wr_ref, rb_ref, xo_ref, row_ref, cls_ref, *, n_lat):
    is_latent = pl.program_id(1) < n_lat
    mod = mod_ref[...]
    half = xo_ref.shape[0] // 2
    for r0 in (0, half):
        rows = slice(r0, r0 + half)
        y_fourier = jnp.where(is_latent, yf_ref[rows, :], yfc_ref[rows, :])
        acc = None
        for n, y in enumerate((yg_ref[rows, :], yn_ref[rows, :], y_fourier, yc_ref[rows, :])):
            z = _dot(y, wb_ref[n])
            gate2 = 1.0 + jnp.tanh(gl_ref[rows, n * D_MODEL:(n + 1) * D_MODEL].astype(F32))
            acc = gate2 * z if acc is None else acc + gate2 * z
        x = (jnp.where(is_latent, xl_ref[rows, :], xc_ref[rows, :])
             + (0.5 * mod[2:3]) * _dot(acc.astype(BF16), wo_ref[...]))
        xo_ref[rows, :] = x
        h2 = _modulated_norm(x, g2_ref[...], mod[3:4], mod[4:5])
        wr = wr_ref[...]
        wr_hi = wr.astype(BF16)
        wr_lo = (wr - wr_hi.astype(F32)).astype(BF16)
        h2_hi = h2.astype(BF16)
        h2_lo = (h2 - h2_hi.astype(F32)).astype(BF16)
        logits_t = _dot_t(wr_hi, h2_hi) + (_dot_t(wr_hi, h2_lo) + _dot_t(wr_lo, h2_hi))
        cls, w_lo, w_hi = _route(logits_t, rb_ref[...])
        cls_ref[:, rows] = jnp.concatenate([cls, jnp.zeros((7, half), F32)], axis=0)
        w_t = jnp.concatenate([w_lo, w_hi, jnp.zeros((ROW_EXTRA - 2, half), F32)], axis=0).T
        row_ref[rows, :] = jnp.concatenate([h2, w_t], axis=1)


def _merge(ys, gl, stream, modtab, g2, w_branch, w_out, layer, w_router_t, router_bias, n_lat, n_tiles):
    x_lat, x_ctx, ctx_block = stream
    b, _, d = x_lat.shape
    tm = ROW_TILE
    row = lambda width: pl.BlockSpec((None, tm, width), lambda bb, i: (bb, i, 0))
    return pl.pallas_call(
        functools.partial(_merge_kernel, n_lat=n_lat),
        grid=(b, n_tiles),
        in_specs=[row(512), row(512)] + _stream_specs(tm, 512, n_lat, 0) + [row(512), row(4096)]
                 + _stream_specs(tm, d, n_lat, ctx_block) + [
                  pl.BlockSpec((None, None, 6, d), lambda bb, i: (bb, i // n_lat, 0, 0)),
                  _const_spec((1, d)),
                  _layer_spec((N_BRANCHES, MIX_WIDTH, d), layer),
                  _layer_spec((d, d), layer),
                  _const_spec((N_EXPERTS, d)),
                  _const_spec((N_EXPERTS, 1))],
        out_specs=[row(d), row(d + ROW_EXTRA),
                   pl.BlockSpec((None, None, 8, tm), lambda bb, i: (bb, i, 0, 0))],
        out_shape=[jax.ShapeDtypeStruct((b, n_tiles * tm, d), F32),
                   jax.ShapeDtypeStruct((b, n_tiles * tm, d + ROW_EXTRA), F32),
                   jax.ShapeDtypeStruct((b, n_tiles, 8, tm), F32)],
        compiler_params=_params(("parallel", "parallel")),
        name="merge_router",
    )(*ys, gl, x_lat, x_ctx, modtab, g2, w_branch, w_out, w_router_t, router_bias)


def _plan_kernel(cls_ref, tri_ref, pos_ref, meta_ref, *, n_chunks):
    width = cls_ref.shape[1]
    kk = lax.broadcasted_iota(jnp.int32, (32, width), 0).astype(F32)

    def onehot(c):
        return jnp.where(cls_ref[c:c + 1, :] == kk, 1.0, 0.0)

    cnt = jnp.zeros((32, 1), F32)
    for c in range(n_chunks):
        cnt = cnt + jnp.sum(onehot(c), axis=1, keepdims=True)
    size = jnp.floor((cnt + (EXPERT_TILE - 1)) * (1.0 / EXPERT_TILE)) * EXPERT_TILE
    starts = []
    acc = jnp.zeros((1, 1), F32)
    for k in range(32):
        starts.append(acc)
        acc = acc + size[k:k + 1]
    base = jnp.concatenate(starts, axis=0)
    run = base
    for c in range(n_chunks):
        oh = onehot(c)
        before = _dot(oh.astype(BF16), tri_ref[...])
        pos_ref[c:c + 1, :] = jnp.sum(oh * (run + before), axis=0, keepdims=True).astype(jnp.int32)
        run = run + jnp.sum(oh, axis=1, keepdims=True)

    end = base + size
    tile_row = lax.broadcasted_iota(jnp.int32, (1, 128), 1).astype(F32) * EXPERT_TILE
    tcls = jnp.sum(jnp.where(end[:N_CLASSES] <= tile_row, 1.0, 0.0), axis=0, keepdims=True)
    tcls = jnp.minimum(tcls, N_CLASSES - 1.0)
    n_pairs = float(len(_PAIRS))
    grp = sum(jnp.where(tcls >= n_pairs * g, 1.0, 0.0) for g in range(1, N_EXPERTS // EXPERTS_PER_GROUP))
    pid = tcls - n_pairs * grp
    lo = sum(jnp.where(pid == float(i), float(a), 0.0) for i, (a, _) in enumerate(_PAIRS))
    hi = sum(jnp.where(pid == float(i), float(b), 0.0) for i, (_, b) in enumerate(_PAIRS))
    used = jnp.broadcast_to(end[N_CLASSES - 1:N_CLASSES] * (1.0 / EXPERT_TILE), (1, 128))
    rows = [grp * EXPERTS_PER_GROUP + lo, grp * EXPERTS_PER_GROUP + hi, used, jnp.zeros((5, 128), F32)]
    meta_ref[...] = jnp.concatenate(rows, axis=0).astype(jnp.int32)


def _plan(cls):
    n_chunks, width = cls.shape
    tri = jnp.asarray(np.triu(np.ones((width, width), np.float32), k=1)).astype(BF16)
    return pl.pallas_call(
        functools.partial(_plan_kernel, n_chunks=n_chunks),
        out_shape=[jax.ShapeDtypeStruct((n_chunks, width), jnp.int32),
                   jax.ShapeDtypeStruct((8, 128), jnp.int32)],
        name="moe_plan",
    )(cls, tri)


def _row_copy(src_ref, src_row, dst_ref, dst_row, sem):
    return pltpu.make_async_copy(src_ref.at[pl.ds(src_row, 1)], dst_ref.at[pl.ds(dst_row, 1)], sem)


def _dispatch_kernel(pos_ref, src_ref, init_ref, dst_ref, sem):
    del init_ref
    n_sub, width = pos_ref.shape
    for q in range(n_sub):
        def issue(r, carry, q=q):
            _row_copy(src_ref, q * width + r, dst_ref, pos_ref[q, r], sem).start()
            return carry
        lax.fori_loop(0, width, issue, 0, unroll=8)

    def drain(r, carry):
        _row_copy(src_ref, 0, dst_ref, 0, sem).wait()
        return carry
    lax.fori_loop(0, n_sub * width, drain, 0, unroll=8)


def _dispatch(pos, rows, n_sorted):
    n_steps, n_sub, width = pos.shape
    t, w = rows.shape
    assert t == n_steps * n_sub * width
    return pl.pallas_call(
        _dispatch_kernel,
        grid=(n_steps,),
        in_specs=[pl.BlockSpec((None, n_sub, width), lambda i: (i, 0, 0), memory_space=pltpu.SMEM),
                  pl.BlockSpec((n_sub * width, w), lambda i: (i, 0)),
                  pl.BlockSpec(memory_space=pl.ANY)],
        out_specs=pl.BlockSpec(memory_space=pl.ANY),
        out_shape=jax.ShapeDtypeStruct((n_sorted, w), F32),
        scratch_shapes=[pltpu.SemaphoreType.DMA(())],
        input_output_aliases={2: 0},
        compiler_params=_params(("arbitrary",)),
        name="moe_dispatch",
    )(pos, rows, jnp.zeros((n_sorted, w), F32))


def _expert_pair_kernel(lo_ref, hi_ref, used_ref, xs_ref, gu_lo_ref, dn_lo_ref, gu_hi_ref, dn_hi_ref, o_ref):
    del lo_ref, hi_ref
    i = pl.program_id(0)

    @pl.when(i < used_ref[0])
    def _():
        x = xs_ref[:, :D_MODEL].astype(BF16)

        def ffn(gu_ref, dn_ref):
            ab = _dot(x, gu_ref[...])
            a = ab[:, :EXPERT_FF]
            return _dot((a * _sigmoid(a) * ab[:, EXPERT_FF:]).astype(BF16), dn_ref[...])

        o_ref[...] = (xs_ref[:, D_MODEL:D_MODEL + 1] * ffn(gu_lo_ref, dn_lo_ref)
                      + xs_ref[:, D_MODEL + 1:D_MODEL + 2] * ffn(gu_hi_ref, dn_hi_ref))

    @pl.when(i >= used_ref[0])
    def _():
        o_ref[...] = jnp.zeros_like(o_ref)


def _expert_pairs(meta, xs, w_gu, w_down, layer):
    n_sorted, w = xs.shape
    d = D_MODEL
    te = EXPERT_TILE
    gu = lambda ref: pl.BlockSpec((None, None, d, 2 * EXPERT_FF),
                                  lambda i, lo, hi, used: (layer, ref(lo, hi)[i], 0, 0))
    dn = lambda ref: pl.BlockSpec((None, None, EXPERT_FF, d),
                                  lambda i, lo, hi, used: (layer, ref(lo, hi)[i], 0, 0))
    first, second = (lambda lo, hi: lo), (lambda lo, hi: hi)
    grid_spec = pltpu.PrefetchScalarGridSpec(
        num_scalar_prefetch=3,
        grid=(n_sorted // te,),
        in_specs=[pl.BlockSpec((te, w), lambda i, lo, hi, used: (i, 0)),
                  gu(first), dn(first), gu(second), dn(second)],
        out_specs=pl.BlockSpec((te, d), lambda i, lo, hi, used: (i, 0)))
    return pl.pallas_call(
        _expert_pair_kernel,
        grid_spec=grid_spec,
        out_shape=jax.ShapeDtypeStruct((n_sorted, d), F32),
        compiler_params=_params(("arbitrary",)),
        name="moe_expert_pairs",
    )(meta[0], meta[1], meta[2, :1], xs, w_gu, w_down, w_gu, w_down)


def _combine_kernel(pos_ref, x_ref, g2_ref, *rest, final):
    if final:
        g_ref, ys_ref, o_ref, z_ref, sem = rest
    else:
        ys_ref, o_ref, z_ref, sem = rest
    n_sub, width = pos_ref.shape
    for q in range(n_sub):
        def issue(r, carry, q=q):
            _row_copy(ys_ref, pos_ref[q, r], z_ref, q * width + r, sem).start()
            return carry
        lax.fori_loop(0, width, issue, 0, unroll=8)

    def drain(r, carry):
        _row_copy(ys_ref, 0, z_ref, 0, sem).wait()
        return carry
    lax.fori_loop(0, n_sub * width, drain, 0, unroll=8)

    for q in range(n_sub):
        rows = slice(q * width, (q + 1) * width)
        x = x_ref[rows, :] + g2_ref[q] * z_ref[rows, :]
        if final:
            ms = jnp.mean(x * x, axis=-1, keepdims=True)
            x = x * lax.rsqrt(ms + EPS) * g_ref[...]
        o_ref[rows, :] = x


def _combine(pos, x, ys, g2, final_g=None):
    n_steps, n_sub, width = pos.shape
    t, d = x.shape
    tm = n_sub * width
    row = pl.BlockSpec((tm, d), lambda i: (i, 0))
    in_specs = [pl.BlockSpec((None, n_sub, width), lambda i: (i, 0, 0), memory_space=pltpu.SMEM),
                row,
                pl.BlockSpec((n_sub, 1, d), lambda i: (i, 0, 0))]
    args = [pos, x, g2]
    if final_g is not None:
        in_specs.append(_const_spec((1, d)))
        args.append(final_g)
    in_specs.append(pl.BlockSpec(memory_space=pl.ANY))
    args.append(ys)
    return pl.pallas_call(
        functools.partial(_combine_kernel, final=final_g is not None),
        grid=(n_steps,),
        in_specs=in_specs,
        out_specs=row,
        out_shape=jax.ShapeDtypeStruct((t, d), F32),
        scratch_shapes=[pltpu.VMEM((tm, d), F32), pltpu.SemaphoreType.DMA(())],
        compiler_params=_params(("arbitrary",)),
        name="moe_combine_final_norm" if final_g is not None else "moe_combine",
    )(*args)


def _moe(rows, cls, xa, modtab, w_gu, w_down, layer, n_lat, final_g=None):
    b, n, w = rows.shape
    d = xa.shape[-1]
    t = b * n
    n_tiles = n // ROW_TILE
    n_chunks = b * n_tiles
    n_sub = DISPATCH_CHUNKS if n_chunks % DISPATCH_CHUNKS == 0 else 1
    n_sorted = t + N_CLASSES * EXPERT_TILE
    assert n_sorted // EXPERT_TILE <= 128
    pos, meta = _plan(cls[:, :, 0, :].reshape(n_chunks, ROW_TILE))
    pos = pos.reshape(n_chunks // n_sub, n_sub, ROW_TILE)
    xs = _dispatch(pos, rows.reshape(t, w), n_sorted)
    ys = _expert_pairs(meta, xs, w_gu, w_down, layer)
    g2 = modtab[:, :, 5, :]
    g2 = jnp.concatenate([jnp.broadcast_to(g2[:, :1], (b, n_lat, d)), g2[:, 1:]], axis=1)[:, :n_tiles]
    out = _combine(pos, xa.reshape(t, d), ys, g2.reshape(n_chunks, 1, d), final_g)
    return out.reshape(b, n, d)


def _rope_tables(seq, ctx):
    t = jnp.arange(seq, dtype=jnp.int32)
    row = (t // GRID_W).astype(F32)
    col = (t % GRID_W).astype(F32)
    n_pairs = HEAD_DIM // 4
    inv_freq = ROPE_THETA ** (-jnp.arange(n_pairs, dtype=F32) / n_pairs)
    ang = jnp.concatenate([row[:, None] * inv_freq, col[:, None] * inv_freq], axis=-1)
    cos = jnp.repeat(jnp.cos(ang), 2, axis=-1)
    sin = jnp.repeat(jnp.sin(ang), 2, axis=-1) * jnp.tile(jnp.array([-1.0, 1.0], F32), HEAD_DIM // 2)
    cos = jnp.concatenate([cos, jnp.ones((ctx, HEAD_DIM), F32)], axis=0)
    sin = jnp.concatenate([sin, jnp.zeros((ctx, HEAD_DIM), F32)], axis=0)
    ck, sk = jnp.tile(cos, (1, GQA_KV_HEADS)), jnp.tile(sin, (1, GQA_KV_HEADS))
    cq, sq = jnp.tile(cos, (1, N_HEADS)) * Q_SCALE, jnp.tile(sin, (1, N_HEADS)) * Q_SCALE
    return cq, sq, ck, sk


def _group_sum_matrix():
    idx = np.arange(MIX_WIDTH) // HEAD_DIM
    return jnp.asarray(idx[:, None] == idx[None, :], dtype=BF16)


def _channel_dft_matrix():
    c = np.arange(FOURIER_GROUP_DIM)
    ang = 2.0 * np.pi * ((c[:, None] * c[None, :]) % FOURIER_GROUP_DIM) / FOURIER_GROUP_DIM
    n_groups = MIX_WIDTH // FOURIER_GROUP_DIM
    eye = np.eye(n_groups)
    m = np.concatenate([np.kron(eye, np.cos(ang)), np.kron(eye, np.sin(ang))], axis=1)
    return jnp.asarray(m, dtype=F32).astype(BF16)


def kernel(x, c, ctx, c_ctx, w_mod, b_mod, norm1_g, norm2_g, w_in, q_norm_g, k_norm_g, na_rpb, conv_w, conv_b,
           conv_norm_g, w_branch, w_out, w_router, router_bias, w_expert_gu, w_expert_down, final_norm_g):
    b, seq, d = x.shape
    n_ctx = ctx.shape[1]
    depth = w_mod.shape[0]
    assert d == D_MODEL and seq % ROW_TILE == 0 and n_ctx == ROW_TILE and seq % n_ctx == 0
    assert seq // GRID_W >= NA_BAND_ROWS and b <= 7
    n_lat = seq // ROW_TILE
    n_all = n_lat + 1

    c_all = jnp.zeros((8, d), F32).at[:b].set(c).at[b].set(c_ctx)
    m = _modulation(c_all, w_mod, b_mod)
    m_lat = m[:, :b].reshape(depth, b, 1, 6, d)
    m_ctx = jnp.broadcast_to(m[:, b].reshape(depth, 1, 1, 6, d), (depth, b, 1, 6, d))
    modtab = jnp.concatenate([m_lat, m_ctx], axis=2)

    tabs = _rope_tables(seq, n_ctx)
    gsum = _group_sum_matrix()
    dcs = _channel_dft_matrix()
    dft_lat = _dft_mats(seq)
    dft_ctx = _dft_mats(n_ctx)
    w_router_t = w_router.T
    rb = router_bias.reshape(N_EXPERTS, 1)

    w_in_b, w_branch_b, w_out_b = w_in.astype(BF16), w_branch.astype(BF16), w_out.astype(BF16)
    w_gu_b, w_down_b = w_expert_gu.astype(BF16), w_expert_down.astype(BF16)
    na_bias = _na_bias(na_rpb, seq)

    stream = (x, ctx, 0)
    xa = None
    for l in range(depth):
        last = l == depth - 1
        n_tiles = n_lat if last else n_all
        qg = jnp.tile(q_norm_g[l], N_HEADS).reshape(1, -1)
        kg = jnp.tile(k_norm_g[l], GQA_KV_HEADS).reshape(1, -1)
        q, k, v, nq, nk, nv, fab, cu, gl = _inproj(
            stream, modtab[l], norm1_g[l].reshape(1, d), w_in_b, l, tabs, (qg, kg, gsum, dcs), n_lat)
        y_gqa = _gqa(q, k, v, seq, n_ctx, n_tiles)
        y_na = _na(nq, nk, nv, na_bias, l, seq, n_ctx, n_tiles)
        y_fn = _fourier_latent(fab, seq, dft_lat)
        y_fn_ctx = y_fn if last else _fourier_ctx(fab, seq, n_ctx, dft_ctx)
        y_cv = _conv(cu, conv_w[l], conv_b[l].reshape(1, -1), conv_norm_g[l].reshape(1, -1), seq, n_tiles)
        xa, rows, cls = _merge((y_gqa, y_na, y_fn, y_fn_ctx, y_cv), gl, stream, modtab[l], norm2_g[l].reshape(1, d),
                               w_branch_b, w_out_b, l, w_router_t, rb, n_lat, n_tiles)
        xa = _moe(rows, cls, xa, modtab[l], w_gu_b, w_down_b, l, n_lat,
                  final_norm_g.reshape(1, d) if last else None)
        stream = (xa, xa, n_lat)
    return xa
```
